```python
import jax, jax.numpy as jnp
from jax import lax
import numpy as np

D_MODEL = 1024
BATCH = 2
SEQ = 8192
DEPTH = 2

GRID_W = 64
D_MIX = 1024
EPS = 1e-6
POOL_WIDTH = 256
POOL_WINDOWS = (2, 4, 8, 16)
POOL_GROUPS = 4
POOL_GDIM = POOL_WIDTH // POOL_GROUPS
ATTN_HEADS = 8
ATTN_KV_HEADS = 2
HEAD_DIM = 64
ATTN_GROUP = ATTN_HEADS // ATTN_KV_HEADS
ATTN_WIDTH = ATTN_HEADS * HEAD_DIM
KV_WIDTH = ATTN_KV_HEADS * HEAD_DIM
Q_BLOCK = 128
ROPE_THETA = 10000.0
GDN_HEADS = 4
GDN_DK = 64
GDN_DV = 64
GDN_QK_WIDTH = GDN_HEADS * GDN_DK
GDN_WIDTH = GDN_HEADS * GDN_DV
GDN_CONV = 5
GDN_CHUNK = 64
N_DIR = 2
D_FF = 2816
FFN_CONV = 3
IN_SIZES = (POOL_WIDTH, ATTN_WIDTH, KV_WIDTH, KV_WIDTH, GDN_QK_WIDTH, GDN_QK_WIDTH, GDN_WIDTH, GDN_WIDTH, N_DIR * GDN_HEADS, N_DIR * GDN_HEADS)
D_IN = sum(IN_SIZES)

kernel_name = "hybrid_pool_gqa_gdn_encoder"


def rms_norm(x, w):
    xf = x.astype(jnp.float32)
    y = xf * lax.rsqrt(jnp.mean(xf * xf, axis=-1, keepdims=True) + EPS)
    return (y * w.astype(jnp.float32)).astype(x.dtype)


def l2_norm(x):
    return x * lax.rsqrt(jnp.sum(x * x, axis=-1, keepdims=True) + EPS)


def depthwise_conv_centred(x, w):
    k_w, ch = w.shape
    return lax.conv_general_dilated(x, w[:, None, :].astype(x.dtype), window_strides=(1,), padding=[(k_w // 2, k_w // 2)], dimension_numbers=('NWC', 'WIO', 'NWC'), feature_group_count=ch)


def pool_mixer(u, pool_w, pool_scale):
    b_, s_, _ = u.shape
    uf = u.astype(jnp.float32).reshape(b_, s_, POOL_GROUPS, POOL_GDIM)
    cs = jnp.concatenate([jnp.zeros((b_, 1, POOL_GROUPS, POOL_GDIM), jnp.float32), jnp.cumsum(uf, axis=1)], axis=1)
    t = jnp.arange(s_)[:, None]
    win = jnp.array(POOL_WINDOWS, dtype=jnp.int32)[None, :]
    lo = jnp.clip(t - win // 2, 0, s_)
    hi = jnp.clip(t - win // 2 + win, 0, s_)
    grp = jnp.arange(POOL_GROUPS)[None, :]
    seg = cs[:, hi, grp] - cs[:, lo, grp]
    mean = seg / (hi - lo).astype(jnp.float32)[None, :, :, None]
    pooled = mean - uf
    mixed = jnp.einsum('bsgc,gcd->bsgd', pooled, pool_w.astype(jnp.float32))
    return (mixed.reshape(b_, s_, POOL_WIDTH) * pool_scale.astype(jnp.float32)).astype(u.dtype)


def axial_rope_tables(seq_len):
    rows = seq_len // GRID_W
    row_ids = jnp.repeat(jnp.arange(rows), GRID_W)
    col_ids = jnp.tile(jnp.arange(GRID_W), rows)
    pos = jnp.stack([row_ids, col_ids], axis=-1).astype(jnp.float32)
    axis_dim = HEAD_DIM // 2
    inv_freq = ROPE_THETA ** (-jnp.arange(0, axis_dim, 2, dtype=jnp.float32) / axis_dim)
    ang = pos[:, :, None] * inv_freq
    return jnp.cos(ang), jnp.sin(ang)


def apply_axial_rope(x, cos, sin):
    b_, s_, h_, _ = x.shape
    xr = x.astype(jnp.float32).reshape(b_, s_, h_, 2, 2, HEAD_DIM // 4)
    x1, x2 = xr[..., 0, :], xr[..., 1, :]
    c = cos[None, :, None]
    s = sin[None, :, None]
    out = jnp.stack([x1 * c - x2 * s, x2 * c + x1 * s], axis=-2)
    return out.reshape(b_, s_, h_, HEAD_DIM).astype(x.dtype)


def block_attention(q, k, v):
    b_, s_, hkv, g_, d_ = q.shape
    nb = s_ // Q_BLOCK
    qb = jnp.moveaxis(q.reshape(b_, nb, Q_BLOCK, hkv, g_, d_), 1, 0)
    scale = d_ ** -0.5

    def one_block(qi):
        sc = jnp.einsum('bqhgd,bkhd->bhgqk', qi, k).astype(jnp.float32) * scale
        p = jax.nn.softmax(sc, axis=-1).astype(v.dtype)
        return jnp.einsum('bhgqk,bkhd->bqhgd', p, v)

    ob = lax.map(one_block, qb)
    return jnp.moveaxis(ob, 0, 1).reshape(b_, s_, hkv * g_ * d_)


def chunk_gated_delta_rule(q, k, v, g, beta):
    b_, s_, h_, dk = q.shape
    dv = v.shape[-1]
    c_ = GDN_CHUNK
    nc = s_ // c_

    def vec_chunks(t):
        return jnp.moveaxis(t.reshape(b_, nc, c_, h_, t.shape[-1]), 3, 1)

    def scal_chunks(t):
        return jnp.moveaxis(t.reshape(b_, nc, c_, h_), 3, 1)

    q = vec_chunks(q * dk ** -0.5)
    k = vec_chunks(k)
    v = vec_chunks(v)
    beta = scal_chunks(beta)
    g = jnp.cumsum(scal_chunks(g), axis=-1)
    incl = jnp.tril(jnp.ones((c_, c_), dtype=bool))
    strict = jnp.tril(jnp.ones((c_, c_), dtype=bool), k=-1)
    decay = jnp.exp(jnp.where(incl, g[..., :, None] - g[..., None, :], -jnp.inf))
    k_beta = k * beta[..., None]
    a_mat = jnp.where(strict, jnp.einsum('bhnid,bhnjd->bhnij', k_beta, k) * decay, 0.0)
    eye = jnp.eye(c_, dtype=q.dtype)
    t_mat = lax.linalg.triangular_solve(a_mat + eye, jnp.broadcast_to(eye, a_mat.shape), left_side=True, lower=True)
    u = t_mat @ (v * beta[..., None])
    w = t_mat @ (k_beta * jnp.exp(g)[..., None])
    qk = jnp.einsum('bhnid,bhnjd->bhnij', q, k) * decay
    q_g = q * jnp.exp(g)[..., None]
    g_last = g[..., -1]
    k_g = k * jnp.exp(g_last[..., None] - g)[..., None]
    xs = (jnp.moveaxis(u, 2, 0), jnp.moveaxis(w, 2, 0), jnp.moveaxis(qk, 2, 0), jnp.moveaxis(q_g, 2, 0), jnp.moveaxis(k_g, 2, 0), jnp.moveaxis(jnp.exp(g_last), 2, 0))

    def step(state, inp):
        u_n, w_n, qk_n, qg_n, kg_n, gl_n = inp
        v_new = u_n - jnp.einsum('bhcd,bhde->bhce', w_n, state)
        o_n = jnp.einsum('bhcd,bhde->bhce', qg_n, state) + jnp.einsum('bhij,bhje->bhie', qk_n, v_new)
        state = state * gl_n[..., None, None] + jnp.einsum('bhcd,bhce->bhde', kg_n, v_new)
        return state, o_n

    state0 = jnp.zeros((b_, h_, dk, dv), jnp.float32)
    _, o = lax.scan(step, state0, xs)
    return jnp.transpose(o, (1, 0, 3, 2, 4)).reshape(b_, s_, h_, dv)


def gdn_mixer(qc, kc, vc, z, b_in, a_in, conv_w, a_log, dt_bias, norm_w):
    b_, s_, _ = qc.shape
    qkv = jax.nn.silu(depthwise_conv_centred(jnp.concatenate([qc, kc, vc], axis=-1), conv_w)).astype(jnp.float32)
    q, k, v = jnp.split(qkv, [GDN_QK_WIDTH, 2 * GDN_QK_WIDTH], axis=-1)
    q = l2_norm(q.reshape(b_, s_, GDN_HEADS, GDN_DK))
    k = l2_norm(k.reshape(b_, s_, GDN_HEADS, GDN_DK))
    v = v.reshape(b_, s_, GDN_HEADS, GDN_DV)
    beta = jax.nn.sigmoid(b_in.astype(jnp.float32).reshape(b_, s_, N_DIR, GDN_HEADS))
    a_f = a_in.astype(jnp.float32).reshape(b_, s_, N_DIR, GDN_HEADS)
    g = -jnp.exp(a_log.astype(jnp.float32)) * jax.nn.softplus(a_f + dt_bias.astype(jnp.float32))
    o_fwd = chunk_gated_delta_rule(q, k, v, g[:, :, 0], beta[:, :, 0])
    o_bwd = jnp.flip(chunk_gated_delta_rule(jnp.flip(q, 1), jnp.flip(k, 1), jnp.flip(v, 1), jnp.flip(g[:, :, 1], 1), jnp.flip(beta[:, :, 1], 1)), 1)
    zf = z.astype(jnp.float32).reshape(b_, s_, GDN_HEADS, GDN_DV)
    o = rms_norm(o_fwd + o_bwd, norm_w) * jax.nn.silu(zf)
    return o.reshape(b_, s_, GDN_WIDTH).astype(qc.dtype)


def token_mixer(h, w_in, pool_w, pool_scale, q_norm, k_norm, gdn_conv, gdn_a_log, gdn_dt_bias, gdn_norm, w_out, cos, sin):
    b_, s_, _ = h.shape
    proj = h @ w_in
    split_idx = np.cumsum(IN_SIZES)[:-1].tolist()
    u_pool, qa, ka, va, qc, kc, vc, z, b_in, a_in = jnp.split(proj, split_idx, axis=-1)
    y_pool = pool_mixer(u_pool, pool_w, pool_scale)
    q = apply_axial_rope(rms_norm(qa.reshape(b_, s_, ATTN_HEADS, HEAD_DIM), q_norm), cos, sin)
    k = apply_axial_rope(rms_norm(ka.reshape(b_, s_, ATTN_KV_HEADS, HEAD_DIM), k_norm), cos, sin)
    v = va.reshape(b_, s_, ATTN_KV_HEADS, HEAD_DIM)
    y_attn = block_attention(q.reshape(b_, s_, ATTN_KV_HEADS, ATTN_GROUP, HEAD_DIM), k, v)
    y_gdn = gdn_mixer(qc, kc, vc, z, b_in, a_in, gdn_conv, gdn_a_log, gdn_dt_bias, gdn_norm)
    y = jnp.concatenate([y_pool, y_attn.astype(h.dtype), y_gdn], axis=-1)
    return y @ w_out


def conv_glu_ffn(h, w_up, conv_w, w_down):
    up = depthwise_conv_centred(h @ w_up, conv_w)
    gate, val = jnp.split(up, 2, axis=-1)
    return (jax.nn.gelu(gate, approximate=True) * val) @ w_down


def setup_inputs(seed: int = 0) -> dict:
    key = jax.random.key(seed)
    ks = jax.random.split(key, 20)
    f32 = jnp.float32
    nrm = lambda k, shape, scale: jax.random.normal(k, shape, f32) * scale
    x = jax.random.normal(ks[0], (BATCH, SEQ, D_MODEL), f32)
    pre_mix_norm = 1.0 + nrm(ks[1], (DEPTH, D_MODEL), 0.02)
    post_mix_norm = 1.0 + nrm(ks[2], (DEPTH, D_MODEL), 0.02)
    pre_ffn_norm = 1.0 + nrm(ks[3], (DEPTH, D_MODEL), 0.02)
    post_ffn_norm = 1.0 + nrm(ks[4], (DEPTH, D_MODEL), 0.02)
    w_in = nrm(ks[5], (DEPTH, D_MODEL, D_IN), D_MODEL ** -0.5)
    pool_w = nrm(ks[6], (DEPTH, POOL_GROUPS, POOL_GDIM, POOL_GDIM), POOL_GDIM ** -0.5)
    pool_scale = 1.0 + nrm(ks[7], (DEPTH, POOL_WIDTH), 0.1)
    q_norm = 1.0 + nrm(ks[8], (DEPTH, HEAD_DIM), 0.02)
    k_norm = 1.0 + nrm(ks[9], (DEPTH, HEAD_DIM), 0.02)
    gdn_conv = nrm(ks[10], (DEPTH, GDN_CONV, 2 * GDN_QK_WIDTH + GDN_WIDTH), GDN_CONV ** -0.5)
    gdn_a_log = jnp.log(jax.random.uniform(ks[11], (DEPTH, N_DIR, GDN_HEADS), f32, 1.0, 16.0))
    dt = jnp.exp(jax.random.uniform(ks[12], (DEPTH, N_DIR, GDN_HEADS), f32, jnp.log(0.001), jnp.log(0.1)))
    gdn_dt_bias = dt + jnp.log(-jnp.expm1(-dt))
    gdn_norm = 1.0 + nrm(ks[13], (DEPTH, GDN_DV), 0.02)
    w_out = nrm(ks[14], (DEPTH, D_MIX, D_MODEL), D_MIX ** -0.5)
    w_up = nrm(ks[15], (DEPTH, D_MODEL, 2 * D_FF), D_MODEL ** -0.5)
    ffn_conv = nrm(ks[16], (DEPTH, FFN_CONV, 2 * D_FF), FFN_CONV ** -0.5)
    w_down = nrm(ks[17], (DEPTH, D_FF, D_MODEL), D_FF ** -0.5)
    return {"x": x, "pre_mix_norm": pre_mix_norm, "post_mix_norm": post_mix_norm, "pre_ffn_norm": pre_ffn_norm, "post_ffn_norm": post_ffn_norm, "w_in": w_in, "pool_w": pool_w, "pool_scale": pool_scale, "q_norm": q_norm, "k_norm": k_norm, "gdn_conv": gdn_conv, "gdn_a_log": gdn_a_log, "gdn_dt_bias": gdn_dt_bias, "gdn_norm": gdn_norm, "w_out": w_out, "w_up": w_up, "ffn_conv": ffn_conv, "w_down": w_down}


def reference(x, pre_mix_norm, post_mix_norm, pre_ffn_norm, post_ffn_norm, w_in, pool_w, pool_scale, q_norm, k_norm, gdn_conv, gdn_a_log, gdn_dt_bias, gdn_norm, w_out, w_up, ffn_conv, w_down):
    seq_len = x.shape[1]
    cos, sin = axial_rope_tables(seq_len)
    h = x
    for l in range(DEPTH):
        mix = token_mixer(rms_norm(h, pre_mix_norm[l]), w_in[l], pool_w[l], pool_scale[l], q_norm[l], k_norm[l], gdn_conv[l], gdn_a_log[l], gdn_dt_bias[l], gdn_norm[l], w_out[l], cos, sin)
        h = h + rms_norm(mix, post_mix_norm[l])
        ff = conv_glu_ffn(rms_norm(h, pre_ffn_norm[l]), w_up[l], ffn_conv[l], w_down[l])
        h = h + rms_norm(ff, post_ffn_norm[l])
    return h
```

```python
import functools
import math

import numpy as np
import jax
import jax.numpy as jnp
from jax import lax
from jax.experimental import pallas as pl
from jax.experimental.pallas import tpu as pltpu

F32 = jnp.float32
BF16 = jnp.bfloat16

D_MODEL = 1024
GRID_W = 64
EPS = 1e-6
POOL_WIDTH = 256
POOL_WINDOWS = (2, 4, 8, 16)
POOL_GROUPS = 4
POOL_GDIM = POOL_WIDTH // POOL_GROUPS
ATTN_HEADS = 8
ATTN_KV_HEADS = 2
HEAD_DIM = 64
ATTN_WIDTH = ATTN_HEADS * HEAD_DIM
KV_WIDTH = ATTN_KV_HEADS * HEAD_DIM
ROPE_THETA = 10000.0
GDN_HEADS = 4
GDN_DK = 64
GDN_WIDTH = GDN_HEADS * GDN_DK
GDN_CONV = 5
GDN_CHUNK = 64
N_DIR = 2
D_FF = 2816
FFN_CONV = 3
D_IN = POOL_WIDTH + ATTN_WIDTH + 2 * KV_WIDTH + 4 * GDN_WIDTH + 2 * N_DIR * GDN_HEADS

LANES = 128
SUBLANES = 8
HALO = SUBLANES
VMEM_LIMIT = 56 * 1024 * 1024

N_CAT = 2176
COL_POOL = 0
COL_QA = COL_POOL + POOL_WIDTH
COL_KA = COL_QA + ATTN_WIDTH
COL_VA = COL_KA + KV_WIDTH
COL_GQKV = COL_VA + KV_WIDTH
COL_Z = COL_GQKV + 3 * GDN_WIDTH
COL_GATES = COL_Z + GDN_WIDTH

Q_SCALE = HEAD_DIM ** -0.5 * math.log2(math.e)


def _dot(a, b):
    return jnp.dot(a, b, preferred_element_type=F32)


def _dot_nt(a, b):
    return lax.dot_general(a, b, (((1,), (1,)), ((), ())), preferred_element_type=F32)


def _split3(x):
    x1 = x.astype(BF16)
    r = x - x1.astype(F32)
    x2 = r.astype(BF16)
    x3 = (r - x2.astype(F32)).astype(BF16)
    return x1, x2, x3


def _sel_dot(m01, x):
    x1, x2, x3 = _split3(x)
    return _dot(m01, x1) + _dot(m01, x2) + _dot(m01, x3)


def _dot_sel(x, m01):
    x1, x2, x3 = _split3(x)
    return _dot(x1, m01) + _dot(x2, m01) + _dot(x3, m01)


def _group_sum(x, bd):
    hi = x.astype(BF16)
    lo = (x - hi.astype(F32)).astype(BF16)
    return _dot(hi, bd) + _dot(lo, bd)


def _rms(x, w):
    ms = jnp.mean(x * x, axis=-1, keepdims=True)
    return x * lax.rsqrt(ms + EPS) * w


def _shift_rows(x, d):
    n = x.shape[0]
    return pltpu.roll(x, (-d) % n, 0)


def _params(sem):
    return pltpu.CompilerParams(dimension_semantics=sem, vmem_limit_bytes=VMEM_LIMIT)


def _halo_specs(tm, width, n_rows):
    per = tm // HALO
    last = n_rows // HALO - 1
    return [
        pl.BlockSpec((HALO, width), lambda i: (jnp.maximum(i * per - 1, 0), 0)),
        pl.BlockSpec((tm, width), lambda i: (i, 0)),
        pl.BlockSpec((HALO, width), lambda i: (jnp.minimum((i + 1) * per, last), 0)),
    ]


def _seq_edges(tiles_per_seq):
    t = pl.program_id(0) % tiles_per_seq
    return t, t == 0, t == tiles_per_seq - 1


def _inproj_kernel(x_ref, nw_ref, w_ref, cos_ref, sin_ref, qw_ref, kw_ref, bd_ref,
                   up_ref, q_ref, k_ref, vt_ref, g_ref, z_ref, gt_ref):
    tm = x_ref.shape[0]
    xn = _rms(x_ref[...], nw_ref[...]).astype(BF16)
    up_ref[...] = _dot(xn, w_ref[:, COL_POOL:COL_QA])

    cos = cos_ref[...]
    sin = sin_ref[...]
    lane = lax.broadcasted_iota(jnp.int32, (tm, LANES), 1)
    first_half = (lane & 16) == 0
    lo64 = lane < HEAD_DIM
    bd = bd_ref[...]

    def head_norm_rope(x, w):
        ss = _group_sum(x * x, bd)
        y = x * lax.rsqrt(ss * (1.0 / HEAD_DIM) + EPS) * w
        partner = jnp.where(first_half, pltpu.roll(y, LANES - 16, 1), pltpu.roll(y, 16, 1))
        return y * cos + partner * sin

    qa = _dot(xn, w_ref[:, COL_QA:COL_KA])
    qw = qw_ref[...] * Q_SCALE
    for s in range(ATTN_WIDTH // LANES):
        sl = slice(s * LANES, (s + 1) * LANES)
        q_ref[:, sl] = head_norm_rope(qa[:, sl], qw).astype(BF16)

    ka = _dot(xn, w_ref[:, COL_KA:COL_VA])
    kr = head_norm_rope(ka, kw_ref[...])
    ks = pltpu.roll(kr, HEAD_DIM, 1)
    k_ref[:, 0:128] = jnp.where(lo64, kr, 0.0).astype(BF16)
    k_ref[:, 128:256] = jnp.where(lo64, 0.0, ks).astype(BF16)
    k_ref[:, 256:384] = jnp.where(lo64, ks, 0.0).astype(BF16)
    k_ref[:, 384:512] = jnp.where(lo64, 0.0, kr).astype(BF16)

    va = _dot(xn, w_ref[:, COL_VA:COL_GQKV])
    vt_ref[...] = va.T.astype(BF16)
    g_ref[...] = _dot(xn, w_ref[:, COL_GQKV:COL_Z])
    z_ref[...] = _dot(xn, w_ref[:, COL_Z:COL_GATES])
    gt_ref[...] = _dot(xn, w_ref[:, COL_GATES:N_CAT])


def _inproj(x2d, norm_w, w_cat, cos_t, sin_t, qw, kw, bd128, seq, tm):
    n_tok = x2d.shape[0]
    tps = seq // tm
    const = lambda i: (0, 0)
    row = lambda i: (i, 0)
    pos = lambda i: (i % tps, 0)
    return pl.pallas_call(
        _inproj_kernel,
        grid=(n_tok // tm,),
        in_specs=[
            pl.BlockSpec((tm, D_MODEL), row),
            pl.BlockSpec((1, D_MODEL), const),
            pl.BlockSpec((D_MODEL, N_CAT), const),
            pl.BlockSpec((tm, LANES), pos),
            pl.BlockSpec((tm, LANES), pos),
            pl.BlockSpec((1, LANES), const),
            pl.BlockSpec((1, LANES), const),
            pl.BlockSpec((LANES, LANES), const),
        ],
        out_specs=[
            pl.BlockSpec((tm, POOL_WIDTH), row),
            pl.BlockSpec((tm, ATTN_WIDTH), row),
            pl.BlockSpec((tm, 4 * LANES), row),
            pl.BlockSpec((KV_WIDTH, tm), lambda i: (0, i)),
            pl.BlockSpec((tm, 3 * GDN_WIDTH), row),
            pl.BlockSpec((tm, GDN_WIDTH), row),
            pl.BlockSpec((tm, LANES), row),
        ],
        out_shape=[
            jax.ShapeDtypeStruct((n_tok, POOL_WIDTH), F32),
            jax.ShapeDtypeStruct((n_tok, ATTN_WIDTH), BF16),
            jax.ShapeDtypeStruct((n_tok, 4 * LANES), BF16),
            jax.ShapeDtypeStruct((KV_WIDTH, n_tok), BF16),
            jax.ShapeDtypeStruct((n_tok, 3 * GDN_WIDTH), F32),
            jax.ShapeDtypeStruct((n_tok, GDN_WIDTH), F32),
            jax.ShapeDtypeStruct((n_tok, LANES), F32),
        ],
        compiler_params=_params(("parallel",)),
        name="inproj",
    )(x2d, norm_w, w_cat, cos_t, sin_t, qw, kw, bd128)


def _pool_kernel(prev_ref, u_ref, next_ref, bdw_ref, sc_ref, o_ref, *, tps, seq):
    tm = u_ref.shape[0]
    t, first, last = _seq_edges(tps)
    u = u_ref[...]
    prev = jnp.where(first, 0.0, prev_ref[...])
    nxt = jnp.where(last, 0.0, next_ref[...])
    ext = jnp.concatenate([prev, u, nxt], axis=0)
    w2 = ext + _shift_rows(ext, -1)
    w4 = _shift_rows(w2, -1) + _shift_rows(w2, 1)
    w8 = _shift_rows(w4, -2) + _shift_rows(w4, 2)
    w16 = _shift_rows(w8, -4) + _shift_rows(w8, 4)
    sl = slice(HALO, HALO + tm)
    grp = lax.broadcasted_iota(jnp.int32, (tm, POOL_WIDTH), 1) // POOL_GDIM
    seg = jnp.where(grp == 0, w2[sl], jnp.where(grp == 1, w4[sl], jnp.where(grp == 2, w8[sl], w16[sl])))
    half = jnp.where(grp == 0, 1, jnp.where(grp == 1, 2, jnp.where(grp == 2, 4, 8)))
    pos = t * tm + lax.broadcasted_iota(jnp.int32, (tm, POOL_WIDTH), 0)
    cnt = (jnp.minimum(pos + half, seq) - jnp.maximum(pos - half, 0)).astype(F32)
    pooled = seg / cnt - u
    o_ref[...] = (_dot(pooled.astype(BF16), bdw_ref[...]) * sc_ref[...]).astype(BF16)


def _pool(u, bdw, scale, seq, tm):
    n_tok = u.shape[0]
    const = lambda i: (0, 0)
    return pl.pallas_call(
        functools.partial(_pool_kernel, tps=seq // tm, seq=seq),
        grid=(n_tok // tm,),
        in_specs=_halo_specs(tm, POOL_WIDTH, n_tok) + [
            pl.BlockSpec((POOL_WIDTH, POOL_WIDTH), const),
            pl.BlockSpec((1, POOL_WIDTH), const),
        ],
        out_specs=pl.BlockSpec((tm, POOL_WIDTH), lambda i: (i, 0)),
        out_shape=jax.ShapeDtypeStruct((n_tok, POOL_WIDTH), BF16),
        compiler_params=_params(("parallel",)),
        name="pool",
    )(u, u, u, bdw, scale)


def _attn_kernel(q_ref, k_ref, vt_ref, o_ref, acc_ref, *, tk):
    tq = q_ref.shape[0]
    seq = k_ref.shape[0]
    group = ATTN_HEADS // ATTN_KV_HEADS
    for j in range(group):
        q = q_ref[:, (j // 2) * LANES:(j // 2 + 1) * LANES]
        kcol = (j % 2) * LANES

        def body(c, carry):
            m, l, acc = carry
            r0 = pl.multiple_of(c * tk, tk)
            k = k_ref[pl.ds(r0, tk), kcol:kcol + LANES]
            s = _dot_nt(k, q)
            m_new = jnp.maximum(m, jnp.max(s, axis=0, keepdims=True))
            alpha = jnp.exp2(m - m_new)
            p = jnp.exp2(s - m_new)
            l_new = alpha * l + jnp.sum(p, axis=0, keepdims=True)
            acc_new = alpha * acc + _dot(vt_ref[:, pl.ds(r0, tk)], p.astype(BF16))
            return m_new, l_new, acc_new

        init = (jnp.full((1, tq), -1e30, F32), jnp.zeros((1, tq), F32), jnp.zeros((HEAD_DIM, tq), F32))
        m, l, acc = lax.fori_loop(0, seq // tk, body, init)
        acc_ref[j * HEAD_DIM:(j + 1) * HEAD_DIM, :] = acc / l
    o_ref[...] = acc_ref[...].T.astype(BF16)


def _attention(q, k4, vt, batch, seq, tq, tk):
    n_tok = q.shape[0]
    nq = seq // tq
    gw = (ATTN_HEADS // ATTN_KV_HEADS) * HEAD_DIM
    return pl.pallas_call(
        functools.partial(_attn_kernel, tk=tk),
        grid=(batch, ATTN_KV_HEADS, nq),
        in_specs=[
            pl.BlockSpec((tq, gw), lambda b, g, i: (b * nq + i, g)),
            pl.BlockSpec((seq, 2 * LANES), lambda b, g, i: (b, g)),
            pl.BlockSpec((HEAD_DIM, seq), lambda b, g, i: (g, b)),
        ],
        out_specs=pl.BlockSpec((tq, gw), lambda b, g, i: (b * nq + i, g)),
        out_shape=jax.ShapeDtypeStruct((n_tok, ATTN_WIDTH), BF16),
        scratch_shapes=[pltpu.VMEM((gw, tq), F32)],
        compiler_params=_params(("parallel", "parallel", "parallel")),
        name="attention",
    )(q, k4, vt)


def _gdn_prep_kernel(prev_ref, x_ref, next_ref, gt_ref, cw_ref, na_ref, dtb_ref, bd_ref, ex_ref,
                     q_ref, k_ref, v_ref, be_ref, gce_ref, gcr_ref, kgt_ref, *, tps):
    tm = x_ref.shape[0]
    _, first, last = _seq_edges(tps)
    prev = jnp.where(first, 0.0, prev_ref[...])
    nxt = jnp.where(last, 0.0, next_ref[...])
    ext = jnp.concatenate([prev, x_ref[...], nxt], axis=0)
    cw = cw_ref[...]
    conv = ext * cw[2:3]
    for kk in (0, 1, 3, 4):
        conv = conv + _shift_rows(ext, kk - GDN_CONV // 2) * cw[kk:kk + 1]
    conv = conv[HALO:HALO + tm]
    act = conv / (1.0 + jnp.exp(-conv))
    bd = bd_ref[...]

    def l2n(x):
        return x * lax.rsqrt(_group_sum(x * x, bd) + EPS)

    q = l2n(act[:, 0:GDN_WIDTH]) * (GDN_DK ** -0.5)
    k = l2n(act[:, GDN_WIDTH:2 * GDN_WIDTH])
    q_ref[...] = q
    k_ref[...] = k
    v_ref[...] = act[:, 2 * GDN_WIDTH:3 * GDN_WIDTH]

    gt = gt_ref[...]
    lane = lax.broadcasted_iota(jnp.int32, (tm, LANES), 1)
    beta = 1.0 / (1.0 + jnp.exp(-gt))
    xs = gt + dtb_ref[...]
    softplus = jnp.maximum(xs, 0.0) + jnp.log1p(jnp.exp(-jnp.abs(xs)))
    g = na_ref[...] * softplus

    r = lax.broadcasted_iota(jnp.int32, (tm, tm), 0)
    c = lax.broadcasted_iota(jnp.int32, (tm, tm), 1)
    same = (r // GDN_CHUNK) == (c // GDN_CHUNK)
    ones_bd = jnp.where(same, 1.0, 0.0).astype(BF16)
    tri_f = jnp.where(same & (c <= r), 1.0, 0.0).astype(BF16)
    tri_b = jnp.where(same & (c >= r), 1.0, 0.0).astype(BF16)
    gc = jnp.where(lane < 12, _sel_dot(tri_f, g), _sel_dot(tri_b, g))
    comp = jnp.where(lane < 8, beta, gc)
    exp_all = _dot_sel(comp, ex_ref[...])
    tot = _sel_dot(ones_bd, g)
    tot_e = _dot_sel(tot, ex_ref[:, 2 * GDN_WIDTH:4 * GDN_WIDTH])

    row = lax.broadcasted_iota(jnp.int32, (tm, GDN_WIDTH), 0)
    col = lax.broadcasted_iota(jnp.int32, (tm, GDN_WIDTH), 1)
    diag = (row % GDN_CHUNK) == (col % GDN_DK)
    for d in range(N_DIR):
        be_ref[d] = exp_all[:, d * GDN_WIDTH:(d + 1) * GDN_WIDTH]
        gce = exp_all[:, (2 + d) * GDN_WIDTH:(3 + d) * GDN_WIDTH]
        gce_ref[d] = gce
        gcr_ref[d] = _sel_dot(ones_bd, jnp.where(diag, gce, 0.0))
        kg = k * jnp.exp(tot_e[:, d * GDN_WIDTH:(d + 1) * GDN_WIDTH] - gce)
        kgt_ref[d] = kg.T


def _gdn_prep(gqkv, gates, conv_w, neg_a, dt_b, bd256, expand, seq, tm):
    n_tok = gqkv.shape[0]
    width = 3 * GDN_WIDTH
    const = lambda i: (0, 0)
    row = lambda i: (i, 0)
    drow = lambda i: (0, i, 0)
    tok = jax.ShapeDtypeStruct((n_tok, GDN_WIDTH), F32)
    dtok = jax.ShapeDtypeStruct((N_DIR, n_tok, GDN_WIDTH), F32)
    return pl.pallas_call(
        functools.partial(_gdn_prep_kernel, tps=seq // tm),
        grid=(n_tok // tm,),
        in_specs=_halo_specs(tm, width, n_tok) + [
            pl.BlockSpec((tm, LANES), row),
            pl.BlockSpec((GDN_CONV, width), const),
            pl.BlockSpec((1, LANES), const),
            pl.BlockSpec((1, LANES), const),
            pl.BlockSpec((GDN_WIDTH, GDN_WIDTH), const),
            pl.BlockSpec((LANES, 4 * GDN_WIDTH), const),
        ],
        out_specs=[
            pl.BlockSpec((tm, GDN_WIDTH), row),
            pl.BlockSpec((tm, GDN_WIDTH), row),
            pl.BlockSpec((tm, GDN_WIDTH), row),
            pl.BlockSpec((N_DIR, tm, GDN_WIDTH), drow),
            pl.BlockSpec((N_DIR, tm, GDN_WIDTH), drow),
            pl.BlockSpec((N_DIR, tm, GDN_WIDTH), drow),
            pl.BlockSpec((N_DIR, GDN_WIDTH, tm), lambda i: (0, 0, i)),
        ],
        out_shape=[tok, tok, tok, dtok, dtok, dtok,
                   jax.ShapeDtypeStruct((N_DIR, GDN_WIDTH, n_tok), F32)],
        compiler_params=_params(("parallel",)),
        name="gdn_prep",
    )(gqkv, gqkv, gqkv, gates, conv_w, neg_a, dt_b, bd256, expand)


def _gdn_scan_kernel(qf_ref, kf_ref, vf_ref, bef_ref, gcef_ref, gcrf_ref, kgtf_ref,
                     qb_ref, kb_ref, vb_ref, beb_ref, gceb_ref, gcrb_ref, kgtb_ref,
                     of_ref, ob_ref, state_ref, *, chunks):
    cs = GDN_CHUNK
    w = GDN_WIDTH

    @pl.when(pl.program_id(1) == 0)
    def _():
        state_ref[...] = jnp.zeros_like(state_ref)

    lane_h = lax.broadcasted_iota(jnp.int32, (cs, w), 1) // GDN_DK
    heads = [lane_h == h for h in range(GDN_HEADS)]
    ti = lax.broadcasted_iota(jnp.int32, (cs, w), 0)
    tj = lax.broadcasted_iota(jnp.int32, (cs, w), 1) % cs
    eye = (ti == tj).astype(F32)
    sr = lax.broadcasted_iota(jnp.int32, (w, w), 0) // GDN_DK
    sc = lax.broadcasted_iota(jnp.int32, (w, w), 1) // GDN_DK
    state_mask = sr == sc
    rows_all = lax.broadcasted_iota(jnp.int32, (w, chunks * cs), 1) // cs

    def blockdiag(y):
        return jnp.concatenate([jnp.where(hm, y, 0.0) for hm in heads], axis=0).astype(BF16)

    def one_chunk(d, ci, refs, o_ref):
        q_ref, k_ref, v_ref, be_ref, gce_ref, gcr_ref, kgt_ref = refs
        rs = slice(ci * cs, (ci + 1) * cs)
        q = q_ref[rs, :]
        k = k_ref[rs, :]
        v = v_ref[rs, :]
        be = be_ref[0, rs, :]
        gce = gce_ref[0, rs, :]
        gcr = gcr_ref[0, rs, :]
        incl = (ti >= tj) if d == 0 else (ti <= tj)
        strict = (ti > tj) if d == 0 else (ti < tj)
        kb = k * be
        kq = _dot_nt(jnp.concatenate([kb, q], axis=0).astype(BF16), blockdiag(k))
        decay = jnp.where(incl, jnp.exp(jnp.where(incl, gce - gcr, 0.0)), 0.0)
        a = jnp.where(strict, kq[0:cs] * decay, 0.0)
        qkm = kq[cs:2 * cs] * decay
        pw = _dot(a.astype(BF16), blockdiag(a))
        t = eye - a
        for step in range(5):
            bdp = blockdiag(pw)
            if step < 4:
                both = _dot(jnp.concatenate([pw, t], axis=0).astype(BF16), bdp)
                pw = both[0:cs]
                t = t + both[cs:2 * cs]
            else:
                t = t + _dot(t.astype(BF16), bdp)
        eg = jnp.exp(gce)
        rhs = jnp.concatenate([blockdiag(v * be), blockdiag(kb * eg)], axis=1)
        uw = _dot(t.astype(BF16), rhs)
        u = uw[:, 0:w]
        wk = uw[:, w:2 * w]
        state = state_ref[d]
        ws = _dot(jnp.concatenate([wk, q * eg], axis=0).astype(BF16), state.astype(BF16))
        vn = u - ws[0:cs]
        o_ref[rs, :] = ws[cs:2 * cs] + _dot(qkm.astype(BF16), blockdiag(vn))
        kgt = jnp.where(rows_all == ci, kgt_ref[0], 0.0).astype(BF16)
        zeros = jnp.zeros((cs, w), F32)
        vn_rows = jnp.concatenate([vn if cc == ci else zeros for cc in range(chunks)], axis=0).astype(BF16)
        edge = (ci + 1) * cs - 1 if d == 0 else ci * cs
        gl = jnp.exp(gce_ref[0, edge:edge + 1, :])
        state_ref[d] = state * gl + jnp.where(state_mask, _dot(kgt, vn_rows), 0.0)

    fwd = (qf_ref, kf_ref, vf_ref, bef_ref, gcef_ref, gcrf_ref, kgtf_ref)
    bwd = (qb_ref, kb_ref, vb_ref, beb_ref, gceb_ref, gcrb_ref, kgtb_ref)
    for ci in range(chunks):
        one_chunk(0, ci, fwd, of_ref)
        one_chunk(1, chunks - 1 - ci, bwd, ob_ref)


def _gdn_scan(q, k, v, be, gce, gcr, kgt, batch, seq, chunks):
    n_tok = q.shape[0]
    rows = chunks * GDN_CHUNK
    ng = seq // rows
    w = GDN_WIDTH

    def tok(rev):
        if rev:
            return lambda b, n: (b * ng + ng - 1 - n, 0)
        return lambda b, n: (b * ng + n, 0)

    def dtok(d, rev):
        f = tok(rev)
        return lambda b, n: (d, f(b, n)[0], 0)

    def dlane(d, rev):
        f = tok(rev)
        return lambda b, n: (d, 0, f(b, n)[0])

    def specs(d):
        rev = d == 1
        return [
            pl.BlockSpec((rows, w), tok(rev)),
            pl.BlockSpec((rows, w), tok(rev)),
            pl.BlockSpec((rows, w), tok(rev)),
            pl.BlockSpec((1, rows, w), dtok(d, rev)),
            pl.BlockSpec((1, rows, w), dtok(d, rev)),
            pl.BlockSpec((1, rows, w), dtok(d, rev)),
            pl.BlockSpec((1, w, rows), dlane(d, rev)),
        ]

    out = jax.ShapeDtypeStruct((n_tok, w), F32)
    return pl.pallas_call(
        functools.partial(_gdn_scan_kernel, chunks=chunks),
        grid=(batch, ng),
        in_specs=specs(0) + specs(1),
        out_specs=[pl.BlockSpec((rows, w), tok(False)), pl.BlockSpec((rows, w), tok(True))],
        out_shape=[out, out],
        scratch_shapes=[pltpu.VMEM((N_DIR, w, w), F32)],
        compiler_params=_params(("parallel", "arbitrary")),
        name="gdn_scan",
    )(q, k, v, be, gce, gcr, kgt, q, k, v, be, gce, gcr, kgt)


def _outproj_kernel(yp_ref, ya_ref, of_ref, ob_ref, z_ref, h_ref, gw_ref, bd_ref,
                    wp_ref, wa_ref, wg_ref, nw_ref, o_ref):
    o = of_ref[...] + ob_ref[...]
    ms = _group_sum(o * o, bd_ref[...]) * (1.0 / GDN_DK)
    z = z_ref[...]
    yg = o * lax.rsqrt(ms + EPS) * gw_ref[...] * (z / (1.0 + jnp.exp(-z)))
    mix = _dot(yp_ref[...], wp_ref[...]) + _dot(ya_ref[...], wa_ref[...]) + _dot(yg.astype(BF16), wg_ref[...])
    o_ref[...] = h_ref[...] + _rms(mix, nw_ref[...])


def _outproj(y_pool, y_attn, o_f, o_b, z, h, gdn_nw, bd256, w_out, post_w, tm):
    n_tok = h.shape[0]
    const = lambda i: (0, 0)
    row = lambda i: (i, 0)
    a0 = POOL_WIDTH
    a1 = POOL_WIDTH + ATTN_WIDTH
    return pl.pallas_call(
        _outproj_kernel,
        grid=(n_tok // tm,),
        in_specs=[
            pl.BlockSpec((tm, POOL_WIDTH), row),
            pl.BlockSpec((tm, ATTN_WIDTH), row),
            pl.BlockSpec((tm, GDN_WIDTH), row),
            pl.BlockSpec((tm, GDN_WIDTH), row),
            pl.BlockSpec((tm, GDN_WIDTH), row),
            pl.BlockSpec((tm, D_MODEL), row),
            pl.BlockSpec((1, GDN_WIDTH), const),
            pl.BlockSpec((GDN_WIDTH, GDN_WIDTH), const),
            pl.BlockSpec((POOL_WIDTH, D_MODEL), const),
            pl.BlockSpec((ATTN_WIDTH, D_MODEL), const),
            pl.BlockSpec((GDN_WIDTH, D_MODEL), const),
            pl.BlockSpec((1, D_MODEL), const),
        ],
        out_specs=pl.BlockSpec((tm, D_MODEL), row),
        out_shape=jax.ShapeDtypeStruct((n_tok, D_MODEL), F32),
        compiler_params=_params(("parallel",)),
        name="outproj",
    )(y_pool, y_attn, o_f, o_b, z, h, gdn_nw, bd256, w_out[:a0], w_out[a0:a1], w_out[a1:], post_w)


def _ffn_kernel(prev_ref, h_ref, next_ref, prew_ref, wg_ref, wv_ref, cg_ref, cv_ref, wd_ref, postw_ref,
                o_ref, act_ref, *, tps, chunk):
    tm = h_ref.shape[0]
    _, first, last = _seq_edges(tps)
    h = h_ref[...]
    prew = prew_ref[...]
    prev = jnp.where(first, 0.0, _rms(prev_ref[...], prew))
    nxt = jnp.where(last, 0.0, _rms(next_ref[...], prew))
    xn = jnp.concatenate([prev, _rms(h, prew), nxt], axis=0).astype(BF16)
    sl = slice(HALO, HALO + tm)

    def conv3(up, cw):
        out = _shift_rows(up, -1) * cw[0:1] + up * cw[1:2] + _shift_rows(up, 1) * cw[2:3]
        return out[sl]

    for c in range(D_FF // chunk):
        cs = slice(c * chunk, (c + 1) * chunk)
        gate = conv3(_dot(xn, wg_ref[:, cs]), cg_ref[:, cs])
        val = conv3(_dot(xn, wv_ref[:, cs]), cv_ref[:, cs])
        inner = 0.7978845608028654 * (gate + 0.044715 * (gate * gate * gate))
        act_ref[:, cs] = (0.5 * gate * (1.0 + jnp.tanh(inner)) * val).astype(BF16)
    ff = _dot(act_ref[...], wd_ref[...])
    o_ref[...] = h + _rms(ff, postw_ref[...])


def _ffn(h, pre_w, w_gate, w_val, c_gate, c_val, w_down, post_w, seq, tm, chunk):
    n_tok = h.shape[0]
    const = lambda i: (0, 0)
    return pl.pallas_call(
        functools.partial(_ffn_kernel, tps=seq // tm, chunk=chunk),
        grid=(n_tok // tm,),
        in_specs=_halo_specs(tm, D_MODEL, n_tok) + [
            pl.BlockSpec((1, D_MODEL), const),
            pl.BlockSpec((D_MODEL, D_FF), const),
            pl.BlockSpec((D_MODEL, D_FF), const),
            pl.BlockSpec((FFN_CONV, D_FF), const),
            pl.BlockSpec((FFN_CONV, D_FF), const),
            pl.BlockSpec((D_FF, D_MODEL), const),
            pl.BlockSpec((1, D_MODEL), const),
        ],
        out_specs=pl.BlockSpec((tm, D_MODEL), lambda i: (i, 0)),
        out_shape=jax.ShapeDtypeStruct((n_tok, D_MODEL), F32),
        scratch_shapes=[pltpu.VMEM((tm, D_FF), BF16)],
        compiler_params=_params(("parallel",)),
        name="ffn",
    )(h, h, h, pre_w, w_gate, w_val, c_gate, c_val, w_down, post_w)


def _rope_tables(seq):
    t = np.arange(seq)
    pos = np.stack([t // GRID_W, t % GRID_W], axis=-1).astype(np.float32)
    axis_dim = HEAD_DIM // 2
    inv_freq = (ROPE_THETA ** (-np.arange(0, axis_dim, 2, dtype=np.float32) / axis_dim)).astype(np.float32)
    ang = pos[:, :, None] * inv_freq
    j = np.arange(LANES) % HEAD_DIM
    a = ang[:, j // 32, j % 16]
    sign = np.where((j % 32) < 16, -1.0, 1.0).astype(np.float32)
    return jnp.asarray(np.cos(a), F32), jnp.asarray(np.sin(a) * sign, F32)


def _head_blockdiag(n):
    i = np.arange(n) // HEAD_DIM
    return jnp.asarray(i[:, None] == i[None, :], BF16)


def _gate_expand():
    e = np.zeros((LANES, 4 * GDN_WIDTH), np.float32)
    for kind in range(2):
        for d in range(N_DIR):
            for hh in range(GDN_HEADS):
                src = kind * N_DIR * GDN_HEADS + d * GDN_HEADS + hh
                dst = (kind * N_DIR + d) * GDN_WIDTH + hh * GDN_DK
                e[src, dst:dst + GDN_DK] = 1.0
    return jnp.asarray(e, BF16)


def _pad_lanes(v, offset):
    out = jnp.zeros((1, LANES), F32)
    return out.at[0, offset:offset + v.shape[0]].set(v.astype(F32))


def kernel(x, pre_mix_norm, post_mix_norm, pre_ffn_norm, post_ffn_norm, w_in, pool_w, pool_scale, q_norm, k_norm, gdn_conv, gdn_a_log, gdn_dt_bias, gdn_norm, w_out, w_up, ffn_conv, w_down):
    batch, seq, _ = x.shape
    n_tok = batch * seq
    depth = w_in.shape[0]
    assert seq % 512 == 0 and seq % GRID_W == 0

    tm_proj = 512
    tm_pool = 512
    tm_gdn = 256
    tm_ffn = 512
    cos_t, sin_t = _rope_tables(seq)
    bd128 = _head_blockdiag(LANES)
    bd256 = _head_blockdiag(GDN_WIDTH)
    expand = _gate_expand()
    n_gate = N_DIR * GDN_HEADS

    h = x.reshape(n_tok, D_MODEL)
    for l in range(depth):
        w_cat = jnp.pad(w_in[l], ((0, 0), (0, N_CAT - D_IN))).astype(BF16)
        qw = jnp.tile(q_norm[l], 2)[None, :]
        kw = jnp.tile(k_norm[l], 2)[None, :]
        u_pool, q, k4, vt, gqkv, z, gates = _inproj(
            h, pre_mix_norm[l][None, :], w_cat, cos_t, sin_t, qw, kw, bd128, seq, tm_proj)

        bdw = jax.scipy.linalg.block_diag(*[pool_w[l, g] for g in range(POOL_GROUPS)]).astype(BF16)
        y_pool = _pool(u_pool, bdw, pool_scale[l][None, :], seq, tm_pool)

        y_attn = _attention(q, k4, vt, batch, seq, 256, 128)

        neg_a = _pad_lanes(-jnp.exp(gdn_a_log[l].reshape(-1)), n_gate)
        dt_b = _pad_lanes(gdn_dt_bias[l].reshape(-1), n_gate)
        gq, gk, gv, be, gce, gcr, kgt = _gdn_prep(gqkv, gates, gdn_conv[l], neg_a, dt_b, bd256, expand, seq, tm_gdn)
        o_f, o_b = _gdn_scan(gq, gk, gv, be, gce, gcr, kgt, batch, seq, 2)

        h = _outproj(y_pool, y_attn, o_f, o_b, z, h, jnp.tile(gdn_norm[l], GDN_HEADS)[None, :], bd256,
                     w_out[l].astype(BF16), post_mix_norm[l][None, :], 512)

        wu = w_up[l].astype(BF16)
        h = _ffn(h, pre_ffn_norm[l][None, :], wu[:, :D_FF], wu[:, D_FF:], ffn_conv[l][:, :D_FF], ffn_conv[l][:, D_FF:],
                 w_down[l].astype(BF16), post_ffn_norm[l][None, :], seq, tm_ffn, 256)
    return h.reshape(batch, seq, D_MODEL)
```

```python
import functools
import math

import numpy as np
import jax
import jax.numpy as jnp
from jax import lax
from jax.experimental import pallas as pl
from jax.experimental.pallas import tpu as pltpu

F32 = jnp.float32
BF16 = jnp.bfloat16

D_MODEL = 1024
GRID_W = 64
EPS = 1e-6
POOL_WIDTH = 256
POOL_WINDOWS = (2, 4, 8, 16)
POOL_GROUPS = 4
POOL_GDIM = POOL_WIDTH // POOL_GROUPS
ATTN_HEADS = 8
ATTN_KV_HEADS = 2
HEAD_DIM = 64
ATTN_WIDTH = ATTN_HEADS * HEAD_DIM
KV_WIDTH = ATTN_KV_HEADS * HEAD_DIM
ROPE_THETA = 10000.0
GDN_HEADS = 4
GDN_DK = 64
GDN_WIDTH = GDN_HEADS * GDN_DK
GDN_CONV = 5
GDN_CHUNK = 64
N_DIR = 2
D_FF = 2816
FFN_CONV = 3
D_IN = POOL_WIDTH + ATTN_WIDTH + 2 * KV_WIDTH + 4 * GDN_WIDTH + 2 * N_DIR * GDN_HEADS

LANES = 128
SUBLANES = 8
HALO = SUBLANES
VMEM_LIMIT = 56 * 1024 * 1024

N_CAT = 2176
COL_POOL = 0
COL_QA = COL_POOL + POOL_WIDTH
COL_KA = COL_QA + ATTN_WIDTH
COL_VA = COL_KA + KV_WIDTH
COL_GQKV = COL_VA + KV_WIDTH
COL_Z = COL_GQKV + 3 * GDN_WIDTH
COL_GATES = COL_Z + GDN_WIDTH

Q_SCALE = HEAD_DIM ** -0.5 * math.log2(math.e)


def _dot(a, b):
    return jnp.dot(a, b, preferred_element_type=F32)


def _dot_nt(a, b):
    return lax.dot_general(a, b, (((1,), (1,)), ((), ())), preferred_element_type=F32)


def _split3(x):
    x1 = x.astype(BF16)
    r = x - x1.astype(F32)
    x2 = r.astype(BF16)
    x3 = (r - x2.astype(F32)).astype(BF16)
    return x1, x2, x3


def _sel_dot(m01, x):
    x1, x2, x3 = _split3(x)
    return _dot(m01, x1) + _dot(m01, x2) + _dot(m01, x3)


def _dot_sel(x, m01):
    x1, x2, x3 = _split3(x)
    return _dot(x1, m01) + _dot(x2, m01) + _dot(x3, m01)


def _group_sum(x, bd):
    hi = x.astype(BF16)
    lo = (x - hi.astype(F32)).astype(BF16)
    return _dot(hi, bd) + _dot(lo, bd)


def _rms(x, w):
    ms = jnp.mean(x * x, axis=-1, keepdims=True)
    return x * lax.rsqrt(ms + EPS) * w


def _shift_rows(x, d):
    n = x.shape[0]
    return pltpu.roll(x, (-d) % n, 0)


def _params(sem):
    return pltpu.CompilerParams(dimension_semantics=sem, vmem_limit_bytes=VMEM_LIMIT)


def _halo_specs(tm, width, n_rows):
    per = tm // HALO
    last = n_rows // HALO - 1
    return [
        pl.BlockSpec((HALO, width), lambda i: (jnp.maximum(i * per - 1, 0), 0)),
        pl.BlockSpec((tm, width), lambda i: (i, 0)),
        pl.BlockSpec((HALO, width), lambda i: (jnp.minimum((i + 1) * per, last), 0)),
    ]


def _seq_edges(tiles_per_seq):
    t = pl.program_id(0) % tiles_per_seq
    return t, t == 0, t == tiles_per_seq - 1


def _inproj_kernel(x_ref, nw_ref, w_ref, cos_ref, sin_ref, qw_ref, kw_ref, bd_ref,
                   up_ref, q_ref, k_ref, vt_ref, g_ref, z_ref, gt_ref):
    tm = x_ref.shape[0]
    xn = _rms(x_ref[...], nw_ref[...]).astype(BF16)
    up_ref[...] = _dot(xn, w_ref[:, COL_POOL:COL_QA])

    cos = cos_ref[...]
    sin = sin_ref[...]
    lane = lax.broadcasted_iota(jnp.int32, (tm, LANES), 1)
    first_half = (lane & 16) == 0
    lo64 = lane < HEAD_DIM
    bd = bd_ref[...]

    def head_norm_rope(x, w):
        ss = _group_sum(x * x, bd)
        y = x * lax.rsqrt(ss * (1.0 / HEAD_DIM) + EPS) * w
        partner = jnp.where(first_half, pltpu.roll(y, LANES - 16, 1), pltpu.roll(y, 16, 1))
        return y * cos + partner * sin

    qa = _dot(xn, w_ref[:, COL_QA:COL_KA])
    qw = qw_ref[...] * Q_SCALE
    for s in range(ATTN_WIDTH // LANES):
        sl = slice(s * LANES, (s + 1) * LANES)
        q_ref[:, sl] = head_norm_rope(qa[:, sl], qw).astype(BF16)

    ka = _dot(xn, w_ref[:, COL_KA:COL_VA])
    kr = head_norm_rope(ka, kw_ref[...])
    ks = pltpu.roll(kr, HEAD_DIM, 1)
    k_ref[:, 0:128] = jnp.where(lo64, kr, 0.0).astype(BF16)
    k_ref[:, 128:256] = jnp.where(lo64, 0.0, ks).astype(BF16)
    k_ref[:, 256:384] = jnp.where(lo64, ks, 0.0).astype(BF16)
    k_ref[:, 384:512] = jnp.where(lo64, 0.0, kr).astype(BF16)

    va = _dot(xn, w_ref[:, COL_VA:COL_GQKV])
    vt_ref[...] = va.T.astype(BF16)
    g_ref[...] = _dot(xn, w_ref[:, COL_GQKV:COL_Z])
    z_ref[...] = _dot(xn, w_ref[:, COL_Z:COL_GATES])
    gt_ref[...] = _dot(xn, w_ref[:, COL_GATES:N_CAT])


def _inproj(x2d, norm_w, w_cat, cos_t, sin_t, qw, kw, bd128, seq, tm):
    n_tok = x2d.shape[0]
    tps = seq // tm
    const = lambda i: (0, 0)
    row = lambda i: (i, 0)
    pos = lambda i: (i % tps, 0)
    return pl.pallas_call(
        _inproj_kernel,
        grid=(n_tok // tm,),
        in_specs=[
            pl.BlockSpec((tm, D_MODEL), row),
            pl.BlockSpec((1, D_MODEL), const),
            pl.BlockSpec((D_MODEL, N_CAT), const),
            pl.BlockSpec((tm, LANES), pos),
            pl.BlockSpec((tm, LANES), pos),
            pl.BlockSpec((1, LANES), const),
            pl.BlockSpec((1, LANES), const),
            pl.BlockSpec((LANES, LANES), const),
        ],
        out_specs=[
            pl.BlockSpec((tm, POOL_WIDTH), row),
            pl.BlockSpec((tm, ATTN_WIDTH), row),
            pl.BlockSpec((tm, 4 * LANES), row),
            pl.BlockSpec((KV_WIDTH, tm), lambda i: (0, i)),
            pl.BlockSpec((tm, 3 * GDN_WIDTH), row),
            pl.BlockSpec((tm, GDN_WIDTH), row),
            pl.BlockSpec((tm, LANES), row),
        ],
        out_shape=[
            jax.ShapeDtypeStruct((n_tok, POOL_WIDTH), F32),
            jax.ShapeDtypeStruct((n_tok, ATTN_WIDTH), BF16),
            jax.ShapeDtypeStruct((n_tok, 4 * LANES), BF16),
            jax.ShapeDtypeStruct((KV_WIDTH, n_tok), BF16),
            jax.ShapeDtypeStruct((n_tok, 3 * GDN_WIDTH), F32),
            jax.ShapeDtypeStruct((n_tok, GDN_WIDTH), F32),
            jax.ShapeDtypeStruct((n_tok, LANES), F32),
        ],
        compiler_params=_params(("parallel",)),
        name="inproj",
    )(x2d, norm_w, w_cat, cos_t, sin_t, qw, kw, bd128)


def _pool_kernel(prev_ref, u_ref, next_ref, bdw_ref, sc_ref, o_ref, *, tps, seq):
    tm = u_ref.shape[0]
    t, first, last = _seq_edges(tps)
    u = u_ref[...]
    prev = jnp.where(first, 0.0, prev_ref[...])
    nxt = jnp.where(last, 0.0, next_ref[...])
    ext = jnp.concatenate([prev, u, nxt], axis=0)
    w2 = ext + _shift_rows(ext, -1)
    w4 = _shift_rows(w2, -1) + _shift_rows(w2, 1)
    w8 = _shift_rows(w4, -2) + _shift_rows(w4, 2)
    w16 = _shift_rows(w8, -4) + _shift_rows(w8, 4)
    sl = slice(HALO, HALO + tm)
    grp = lax.broadcasted_iota(jnp.int32, (tm, POOL_WIDTH), 1) // POOL_GDIM
    seg = jnp.where(grp == 0, w2[sl], jnp.where(grp == 1, w4[sl], jnp.where(grp == 2, w8[sl], w16[sl])))
    half = jnp.where(grp == 0, 1, jnp.where(grp == 1, 2, jnp.where(grp == 2, 4, 8)))
    pos = t * tm + lax.broadcasted_iota(jnp.int32, (tm, POOL_WIDTH), 0)
    cnt = (jnp.minimum(pos + half, seq) - jnp.maximum(pos - half, 0)).astype(F32)
    pooled = seg / cnt - u
    o_ref[...] = (_dot(pooled.astype(BF16), bdw_ref[...]) * sc_ref[...]).astype(BF16)


def _pool(u, bdw, scale, seq, tm):
    n_tok = u.shape[0]
    const = lambda i: (0, 0)
    return pl.pallas_call(
        functools.partial(_pool_kernel, tps=seq // tm, seq=seq),
        grid=(n_tok // tm,),
        in_specs=_halo_specs(tm, POOL_WIDTH, n_tok) + [
            pl.BlockSpec((POOL_WIDTH, POOL_WIDTH), const),
            pl.BlockSpec((1, POOL_WIDTH), const),
        ],
        out_specs=pl.BlockSpec((tm, POOL_WIDTH), lambda i: (i, 0)),
        out_shape=jax.ShapeDtypeStruct((n_tok, POOL_WIDTH), BF16),
        compiler_params=_params(("parallel",)),
        name="pool",
    )(u, u, u, bdw, scale)


def _attn_kernel(q_ref, k_ref, vt_ref, o_ref, acc_ref, m_ref, l_ref, s_ref, bm_ref, *, tk):
    seq = k_ref.shape[0]
    group = ATTN_HEADS // ATTN_KV_HEADS
    n_chunks = seq // tk
    acc_ref[...] = jnp.zeros_like(acc_ref)
    m_ref[...] = jnp.full_like(m_ref, -1e30)
    l_ref[...] = jnp.zeros_like(l_ref)

    def scores(c, buf):
        r0 = pl.multiple_of(c * tk, tk)
        kk = jnp.concatenate([k_ref[pl.ds(r0, tk), 0:LANES], k_ref[pl.ds(r0, tk), LANES:2 * LANES]], axis=0)
        for slab in range(group // 2):
            s2 = _dot_nt(kk, q_ref[:, slab * LANES:(slab + 1) * LANES])
            for half in range(2):
                j = 2 * slab + half
                s = s2[half * tk:(half + 1) * tk]
                s_ref[buf, j] = s
                bm_ref[buf, j] = jnp.max(s, axis=0, keepdims=True)

    def softmax_pv(c, buf):
        r0 = pl.multiple_of(c * tk, tk)
        vt = vt_ref[:, pl.ds(r0, tk)]
        for j in range(group):
            m = m_ref[j]
            m_new = jnp.maximum(m, bm_ref[buf, j])
            alpha = jnp.exp2(m - m_new)
            p = jnp.exp2(s_ref[buf, j] - m_new)
            m_ref[j] = m_new
            l_ref[j] = alpha * l_ref[j] + jnp.sum(p, axis=0, keepdims=True)
            acc_ref[j] = alpha * acc_ref[j] + _dot(vt, p.astype(BF16))

    scores(0, 0)

    def body(i, carry):
        c = 2 * i
        scores(c + 1, 1)
        softmax_pv(c, 0)
        scores(jnp.minimum(c + 2, n_chunks - 1), 0)
        softmax_pv(c + 1, 1)
        return carry

    lax.fori_loop(0, n_chunks // 2, body, 0)
    out = jnp.concatenate([acc_ref[j] / l_ref[j] for j in range(group)], axis=0)
    o_ref[...] = out.T.astype(BF16)


def _attention(q, k4, vt, batch, seq, tq, tk):
    n_tok = q.shape[0]
    nq = seq // tq
    group = ATTN_HEADS // ATTN_KV_HEADS
    gw = group * HEAD_DIM
    assert (seq // tk) % 2 == 0
    return pl.pallas_call(
        functools.partial(_attn_kernel, tk=tk),
        grid=(batch, ATTN_KV_HEADS, nq),
        in_specs=[
            pl.BlockSpec((tq, gw), lambda b, g, i: (b * nq + i, g)),
            pl.BlockSpec((seq, 2 * LANES), lambda b, g, i: (b, g)),
            pl.BlockSpec((HEAD_DIM, seq), lambda b, g, i: (g, b)),
        ],
        out_specs=pl.BlockSpec((tq, gw), lambda b, g, i: (b * nq + i, g)),
        out_shape=jax.ShapeDtypeStruct((n_tok, ATTN_WIDTH), BF16),
        scratch_shapes=[pltpu.VMEM((group, HEAD_DIM, tq), F32),
                        pltpu.VMEM((group, 1, tq), F32),
                        pltpu.VMEM((group, 1, tq), F32),
                        pltpu.VMEM((2, group, tk, tq), F32),
                        pltpu.VMEM((2, group, 1, tq), F32)],
        compiler_params=_params(("parallel", "parallel", "parallel")),
        name="attention",
    )(q, k4, vt)


def _gdn_prep_kernel(prev_ref, x_ref, next_ref, gt_ref, cw_ref, na_ref, dtb_ref, bd_ref, ex_ref,
                     q_ref, k_ref, v_ref, be_ref, gce_ref, gcr_ref, kgt_ref, *, tps):
    tm = x_ref.shape[0]
    _, first, last = _seq_edges(tps)
    prev = jnp.where(first, 0.0, prev_ref[...])
    nxt = jnp.where(last, 0.0, next_ref[...])
    ext = jnp.concatenate([prev, x_ref[...], nxt], axis=0)
    cw = cw_ref[...]
    conv = ext * cw[2:3]
    for kk in (0, 1, 3, 4):
        conv = conv + _shift_rows(ext, kk - GDN_CONV // 2) * cw[kk:kk + 1]
    conv = conv[HALO:HALO + tm]
    act = conv / (1.0 + jnp.exp(-conv))
    bd = bd_ref[...]

    def l2n(x):
        return x * lax.rsqrt(_group_sum(x * x, bd) + EPS)

    q = l2n(act[:, 0:GDN_WIDTH]) * (GDN_DK ** -0.5)
    k = l2n(act[:, GDN_WIDTH:2 * GDN_WIDTH])
    q_ref[...] = q
    k_ref[...] = k
    v_ref[...] = act[:, 2 * GDN_WIDTH:3 * GDN_WIDTH]

    gt = gt_ref[...]
    lane = lax.broadcasted_iota(jnp.int32, (tm, LANES), 1)
    beta = 1.0 / (1.0 + jnp.exp(-gt))
    xs = gt + dtb_ref[...]
    softplus = jnp.maximum(xs, 0.0) + jnp.log1p(jnp.exp(-jnp.abs(xs)))
    g = na_ref[...] * softplus

    r = lax.broadcasted_iota(jnp.int32, (tm, tm), 0)
    c = lax.broadcasted_iota(jnp.int32, (tm, tm), 1)
    same = (r // GDN_CHUNK) == (c // GDN_CHUNK)
    ones_bd = jnp.where(same, 1.0, 0.0).astype(BF16)
    tri_f = jnp.where(same & (c <= r), 1.0, 0.0).astype(BF16)
    tri_b = jnp.where(same & (c >= r), 1.0, 0.0).astype(BF16)
    gc = jnp.where(lane < 12, _sel_dot(tri_f, g), _sel_dot(tri_b, g))
    comp = jnp.where(lane < 8, beta, gc)
    exp_all = _dot_sel(comp, ex_ref[...])
    tot = _sel_dot(ones_bd, g)
    tot_e = _dot_sel(tot, ex_ref[:, 2 * GDN_WIDTH:4 * GDN_WIDTH])

    row = lax.broadcasted_iota(jnp.int32, (tm, GDN_WIDTH), 0)
    col = lax.broadcasted_iota(jnp.int32, (tm, GDN_WIDTH), 1)
    diag = (row % GDN_CHUNK) == (col % GDN_DK)
    for d in range(N_DIR):
        be_ref[d] = exp_all[:, d * GDN_WIDTH:(d + 1) * GDN_WIDTH]
        gce = exp_all[:, (2 + d) * GDN_WIDTH:(3 + d) * GDN_WIDTH]
        gce_ref[d] = gce
        gcr_ref[d] = _sel_dot(ones_bd, jnp.where(diag, gce, 0.0))
        kg = k * jnp.exp(tot_e[:, d * GDN_WIDTH:(d + 1) * GDN_WIDTH] - gce)
        kgt_ref[d] = kg.T


def _gdn_prep(gqkv, gates, conv_w, neg_a, dt_b, bd256, expand, seq, tm):
    n_tok = gqkv.shape[0]
    width = 3 * GDN_WIDTH
    const = lambda i: (0, 0)
    row = lambda i: (i, 0)
    drow = lambda i: (0, i, 0)
    tok = jax.ShapeDtypeStruct((n_tok, GDN_WIDTH), F32)
    dtok = jax.ShapeDtypeStruct((N_DIR, n_tok, GDN_WIDTH), F32)
    return pl.pallas_call(
        functools.partial(_gdn_prep_kernel, tps=seq // tm),
        grid=(n_tok // tm,),
        in_specs=_halo_specs(tm, width, n_tok) + [
            pl.BlockSpec((tm, LANES), row),
            pl.BlockSpec((GDN_CONV, width), const),
            pl.BlockSpec((1, LANES), const),
            pl.BlockSpec((1, LANES), const),
            pl.BlockSpec((GDN_WIDTH, GDN_WIDTH), const),
            pl.BlockSpec((LANES, 4 * GDN_WIDTH), const),
        ],
        out_specs=[
            pl.BlockSpec((tm, GDN_WIDTH), row),
            pl.BlockSpec((tm, GDN_WIDTH), row),
            pl.BlockSpec((tm, GDN_WIDTH), row),
            pl.BlockSpec((N_DIR, tm, GDN_WIDTH), drow),
            pl.BlockSpec((N_DIR, tm, GDN_WIDTH), drow),
            pl.BlockSpec((N_DIR, tm, GDN_WIDTH), drow),
            pl.BlockSpec((N_DIR, GDN_WIDTH, tm), lambda i: (0, 0, i)),
        ],
        out_shape=[tok, tok, tok, dtok, dtok, dtok,
                   jax.ShapeDtypeStruct((N_DIR, GDN_WIDTH, n_tok), F32)],
        compiler_params=_params(("parallel",)),
        name="gdn_prep",
    )(gqkv, gqkv, gqkv, gates, conv_w, neg_a, dt_b, bd256, expand)


def _gdn_scan_kernel(qf_ref, kf_ref, vf_ref, bef_ref, gcef_ref, gcrf_ref, kgtf_ref,
                     qb_ref, kb_ref, vb_ref, beb_ref, gceb_ref, gcrb_ref, kgtb_ref,
                     of_ref, ob_ref, state_ref, *, chunks):
    cs = GDN_CHUNK
    w = GDN_WIDTH

    @pl.when(pl.program_id(1) == 0)
    def _():
        state_ref[...] = jnp.zeros_like(state_ref)

    lane_h = lax.broadcasted_iota(jnp.int32, (cs, w), 1) // GDN_DK
    heads = [lane_h == h for h in range(GDN_HEADS)]
    ti = lax.broadcasted_iota(jnp.int32, (cs, w), 0)
    tj = lax.broadcasted_iota(jnp.int32, (cs, w), 1) % cs
    eye = (ti == tj).astype(F32)
    sr = lax.broadcasted_iota(jnp.int32, (w, w), 0) // GDN_DK
    sc = lax.broadcasted_iota(jnp.int32, (w, w), 1) // GDN_DK
    state_mask = sr == sc
    rows_all = lax.broadcasted_iota(jnp.int32, (w, chunks * cs), 1) // cs

    def blockdiag(y):
        return jnp.concatenate([jnp.where(hm, y, 0.0) for hm in heads], axis=0).astype(BF16)

    def one_chunk(d, ci, refs, o_ref):
        q_ref, k_ref, v_ref, be_ref, gce_ref, gcr_ref, kgt_ref = refs
        rs = slice(ci * cs, (ci + 1) * cs)
        q = q_ref[rs, :]
        k = k_ref[rs, :]
        v = v_ref[rs, :]
        be = be_ref[0, rs, :]
        gce = gce_ref[0, rs, :]
        gcr = gcr_ref[0, rs, :]
        incl = (ti >= tj) if d == 0 else (ti <= tj)
        strict = (ti > tj) if d == 0 else (ti < tj)
        kb = k * be
        kq = _dot_nt(jnp.concatenate([kb, q], axis=0).astype(BF16), blockdiag(k))
        decay = jnp.where(incl, jnp.exp(jnp.where(incl, gce - gcr, 0.0)), 0.0)
        a = jnp.where(strict, kq[0:cs] * decay, 0.0)
        qkm = kq[cs:2 * cs] * decay
        pw = _dot(a.astype(BF16), blockdiag(a))
        t = eye - a
        for step in range(5):
            bdp = blockdiag(pw)
            if step < 4:
                both = _dot(jnp.concatenate([pw, t], axis=0).astype(BF16), bdp)
                pw = both[0:cs]
                t = t + both[cs:2 * cs]
            else:
                t = t + _dot(t.astype(BF16), bdp)
        eg = jnp.exp(gce)
        rhs = jnp.concatenate([blockdiag(v * be), blockdiag(kb * eg)], axis=1)
        uw = _dot(t.astype(BF16), rhs)
        u = uw[:, 0:w]
        wk = uw[:, w:2 * w]
        state = state_ref[d]
        ws = _dot(jnp.concatenate([wk, q * eg], axis=0).astype(BF16), state.astype(BF16))
        vn = u - ws[0:cs]
        o_ref[rs, :] = ws[cs:2 * cs] + _dot(qkm.astype(BF16), blockdiag(vn))
        kgt = jnp.where(rows_all == ci, kgt_ref[0], 0.0).astype(BF16)
        zeros = jnp.zeros((cs, w), F32)
        vn_rows = jnp.concatenate([vn if cc == ci else zeros for cc in range(chunks)], axis=0).astype(BF16)
        edge = (ci + 1) * cs - 1 if d == 0 else ci * cs
        gl = jnp.exp(gce_ref[0, edge:edge + 1, :])
        state_ref[d] = state * gl + jnp.where(state_mask, _dot(kgt, vn_rows), 0.0)

    fwd = (qf_ref, kf_ref, vf_ref, bef_ref, gcef_ref, gcrf_ref, kgtf_ref)
    bwd = (qb_ref, kb_ref, vb_ref, beb_ref, gceb_ref, gcrb_ref, kgtb_ref)
    for ci in range(chunks):
        one_chunk(0, ci, fwd, of_ref)
        one_chunk(1, chunks - 1 - ci, bwd, ob_ref)


def _gdn_scan(q, k, v, be, gce, gcr, kgt, batch, seq, chunks):
    n_tok = q.shape[0]
    rows = chunks * GDN_CHUNK
    ng = seq // rows
    w = GDN_WIDTH

    def tok(rev):
        if rev:
            return lambda b, n: (b * ng + ng - 1 - n, 0)
        return lambda b, n: (b * ng + n, 0)

    def dtok(d, rev):
        f = tok(rev)
        return lambda b, n: (d, f(b, n)[0], 0)

    def dlane(d, rev):
        f = tok(rev)
        return lambda b, n: (d, 0, f(b, n)[0])

    def specs(d):
        rev = d == 1
        return [
            pl.BlockSpec((rows, w), tok(rev)),
            pl.BlockSpec((rows, w), tok(rev)),
            pl.BlockSpec((rows, w), tok(rev)),
            pl.BlockSpec((1, rows, w), dtok(d, rev)),
            pl.BlockSpec((1, rows, w), dtok(d, rev)),
            pl.BlockSpec((1, rows, w), dtok(d, rev)),
            pl.BlockSpec((1, w, rows), dlane(d, rev)),
        ]

    out = jax.ShapeDtypeStruct((n_tok, w), F32)
    return pl.pallas_call(
        functools.partial(_gdn_scan_kernel, chunks=chunks),
        grid=(batch, ng),
        in_specs=specs(0) + specs(1),
        out_specs=[pl.BlockSpec((rows, w), tok(False)), pl.BlockSpec((rows, w), tok(True))],
        out_shape=[out, out],
        scratch_shapes=[pltpu.VMEM((N_DIR, w, w), F32)],
        compiler_params=_params(("parallel", "arbitrary")),
        name="gdn_scan",
    )(q, k, v, be, gce, gcr, kgt, q, k, v, be, gce, gcr, kgt)


def _outproj_kernel(yp_ref, ya_ref, of_ref, ob_ref, z_ref, h_ref, gw_ref, bd_ref,
                    wp_ref, wa_ref, wg_ref, nw_ref, o_ref):
    o = of_ref[...] + ob_ref[...]
    ms = _group_sum(o * o, bd_ref[...]) * (1.0 / GDN_DK)
    z = z_ref[...]
    yg = o * lax.rsqrt(ms + EPS) * gw_ref[...] * (z / (1.0 + jnp.exp(-z)))
    mix = _dot(yp_ref[...], wp_ref[...]) + _dot(ya_ref[...], wa_ref[...]) + _dot(yg.astype(BF16), wg_ref[...])
    o_ref[...] = h_ref[...] + _rms(mix, nw_ref[...])


def _outproj(y_pool, y_attn, o_f, o_b, z, h, gdn_nw, bd256, w_out, post_w, tm):
    n_tok = h.shape[0]
    const = lambda i: (0, 0)
    row = lambda i: (i, 0)
    a0 = POOL_WIDTH
    a1 = POOL_WIDTH + ATTN_WIDTH
    return pl.pallas_call(
        _outproj_kernel,
        grid=(n_tok // tm,),
        in_specs=[
            pl.BlockSpec((tm, POOL_WIDTH), row),
            pl.BlockSpec((tm, ATTN_WIDTH), row),
            pl.BlockSpec((tm, GDN_WIDTH), row),
            pl.BlockSpec((tm, GDN_WIDTH), row),
            pl.BlockSpec((tm, GDN_WIDTH), row),
            pl.BlockSpec((tm, D_MODEL), row),
            pl.BlockSpec((1, GDN_WIDTH), const),
            pl.BlockSpec((GDN_WIDTH, GDN_WIDTH), const),
            pl.BlockSpec((POOL_WIDTH, D_MODEL), const),
            pl.BlockSpec((ATTN_WIDTH, D_MODEL), const),
            pl.BlockSpec((GDN_WIDTH, D_MODEL), const),
            pl.BlockSpec((1, D_MODEL), const),
        ],
        out_specs=pl.BlockSpec((tm, D_MODEL), row),
        out_shape=jax.ShapeDtypeStruct((n_tok, D_MODEL), F32),
        compiler_params=_params(("parallel",)),
        name="outproj",
    )(y_pool, y_attn, o_f, o_b, z, h, gdn_nw, bd256, w_out[:a0], w_out[a0:a1], w_out[a1:], post_w)


def _ffn_kernel(prev_ref, h_ref, next_ref, prew_ref, wg_ref, wv_ref, cg_ref, cv_ref, wd_ref, postw_ref,
                o_ref, act_ref, *, tps, chunk):
    tm = h_ref.shape[0]
    _, first, last = _seq_edges(tps)
    h = h_ref[...]
    prew = prew_ref[...]
    prev = jnp.where(first, 0.0, _rms(prev_ref[...], prew))
    nxt = jnp.where(last, 0.0, _rms(next_ref[...], prew))
    xn = jnp.concatenate([prev, _rms(h, prew), nxt], axis=0).astype(BF16)
    sl = slice(HALO, HALO + tm)

    def conv3(up, cw):
        out = _shift_rows(up, -1) * cw[0:1] + up * cw[1:2] + _shift_rows(up, 1) * cw[2:3]
        return out[sl]

    for c in range(D_FF // chunk):
        cs = slice(c * chunk, (c + 1) * chunk)
        gate = conv3(_dot(xn, wg_ref[:, cs]), cg_ref[:, cs])
        val = conv3(_dot(xn, wv_ref[:, cs]), cv_ref[:, cs])
        inner = 0.7978845608028654 * (gate + 0.044715 * (gate * gate * gate))
        act_ref[:, cs] = (0.5 * gate * (1.0 + jnp.tanh(inner)) * val).astype(BF16)
    ff = _dot(act_ref[...], wd_ref[...])
    o_ref[...] = h + _rms(ff, postw_ref[...])


def _ffn(h, pre_w, w_gate, w_val, c_gate, c_val, w_down, post_w, seq, tm, chunk):
    n_tok = h.shape[0]
    const = lambda i: (0, 0)
    return pl.pallas_call(
        functools.partial(_ffn_kernel, tps=seq // tm, chunk=chunk),
        grid=(n_tok // tm,),
        in_specs=_halo_specs(tm, D_MODEL, n_tok) + [
            pl.BlockSpec((1, D_MODEL), const),
            pl.BlockSpec((D_MODEL, D_FF), const),
            pl.BlockSpec((D_MODEL, D_FF), const),
            pl.BlockSpec((FFN_CONV, D_FF), const),
            pl.BlockSpec((FFN_CONV, D_FF), const),
            pl.BlockSpec((D_FF, D_MODEL), const),
            pl.BlockSpec((1, D_MODEL), const),
        ],
        out_specs=pl.BlockSpec((tm, D_MODEL), lambda i: (i, 0)),
        out_shape=jax.ShapeDtypeStruct((n_tok, D_MODEL), F32),
        scratch_shapes=[pltpu.VMEM((tm, D_FF), BF16)],
        compiler_params=_params(("parallel",)),
        name="ffn",
    )(h, h, h, pre_w, w_gate, w_val, c_gate, c_val, w_down, post_w)


def _rope_tables(seq):
    t = np.arange(seq)
    pos = np.stack([t // GRID_W, t % GRID_W], axis=-1).astype(np.float32)
    axis_dim = HEAD_DIM // 2
    inv_freq = (ROPE_THETA ** (-np.arange(0, axis_dim, 2, dtype=np.float32) / axis_dim)).astype(np.float32)
    ang = pos[:, :, None] * inv_freq
    j = np.arange(LANES) % HEAD_DIM
    a = ang[:, j // 32, j % 16]
    sign = np.where((j % 32) < 16, -1.0, 1.0).astype(np.float32)
    return jnp.asarray(np.cos(a), F32), jnp.asarray(np.sin(a) * sign, F32)


def _head_blockdiag(n):
    i = np.arange(n) // HEAD_DIM
    return jnp.asarray(i[:, None] == i[None, :], BF16)


def _gate_expand():
    e = np.zeros((LANES, 4 * GDN_WIDTH), np.float32)
    for kind in range(2):
        for d in range(N_DIR):
            for hh in range(GDN_HEADS):
                src = kind * N_DIR * GDN_HEADS + d * GDN_HEADS + hh
                dst = (kind * N_DIR + d) * GDN_WIDTH + hh * GDN_DK
                e[src, dst:dst + GDN_DK] = 1.0
    return jnp.asarray(e, BF16)


def _pad_lanes(v, offset):
    out = jnp.zeros((1, LANES), F32)
    return out.at[0, offset:offset + v.shape[0]].set(v.astype(F32))


def kernel(x, pre_mix_norm, post_mix_norm, pre_ffn_norm, post_ffn_norm, w_in, pool_w, pool_scale, q_norm, k_norm, gdn_conv, gdn_a_log, gdn_dt_bias, gdn_norm, w_out, w_up, ffn_conv, w_down):
    batch, seq, _ = x.shape
    n_tok = batch * seq
    depth = w_in.shape[0]
    assert seq % 512 == 0 and seq % GRID_W == 0

    tm_proj = 512
    tm_pool = 512
    tm_gdn = 256
    tm_ffn = 512
    cos_t, sin_t = _rope_tables(seq)
    bd128 = _head_blockdiag(LANES)
    bd256 = _head_blockdiag(GDN_WIDTH)
    expand = _gate_expand()
    n_gate = N_DIR * GDN_HEADS

    h = x.reshape(n_tok, D_MODEL)
    for l in range(depth):
        w_cat = jnp.pad(w_in[l], ((0, 0), (0, N_CAT - D_IN))).astype(BF16)
        qw = jnp.tile(q_norm[l], 2)[None, :]
        kw = jnp.tile(k_norm[l], 2)[None, :]
        u_pool, q, k4, vt, gqkv, z, gates = _inproj(
            h, pre_mix_norm[l][None, :], w_cat, cos_t, sin_t, qw, kw, bd128, seq, tm_proj)

        bdw = jax.scipy.linalg.block_diag(*[pool_w[l, g] for g in range(POOL_GROUPS)]).astype(BF16)
        y_pool = _pool(u_pool, bdw, pool_scale[l][None, :], seq, tm_pool)

        y_attn = _attention(q, k4, vt, batch, seq, 256, 256)

        neg_a = _pad_lanes(-jnp.exp(gdn_a_log[l].reshape(-1)), n_gate)
        dt_b = _pad_lanes(gdn_dt_bias[l].reshape(-1), n_gate)
        gq, gk, gv, be, gce, gcr, kgt = _gdn_prep(gqkv, gates, gdn_conv[l], neg_a, dt_b, bd256, expand, seq, tm_gdn)
        o_f, o_b = _gdn_scan(gq, gk, gv, be, gce, gcr, kgt, batch, seq, 2)

        h = _outproj(y_pool, y_attn, o_f, o_b, z, h, jnp.tile(gdn_norm[l], GDN_HEADS)[None, :], bd256,
                     w_out[l].astype(BF16), post_mix_norm[l][None, :], 512)

        wu = w_up[l].astype(BF16)
        h = _ffn(h, pre_ffn_norm[l][None, :], wu[:, :D_FF], wu[:, D_FF:], ffn_conv[l][:, :D_FF], ffn_conv[l][:, D_FF:],
                 w_down[l].astype(BF16), post_ffn_norm[l][None, :], seq, tm_ffn, 256)
    return h.reshape(batch, seq, D_MODEL)
```

```python
import functools
import math

import numpy as np
import jax
import jax.numpy as jnp
from jax import lax
from jax.experimental import pallas as pl
from jax.experimental.pallas import tpu as pltpu

F32 = jnp.float32
BF16 = jnp.bfloat16

D_MODEL = 1024
GRID_W = 64
EPS = 1e-6
POOL_WIDTH = 256
POOL_WINDOWS = (2, 4, 8, 16)
POOL_GROUPS = 4
POOL_GDIM = POOL_WIDTH // POOL_GROUPS
ATTN_HEADS = 8
ATTN_KV_HEADS = 2
HEAD_DIM = 64
ATTN_WIDTH = ATTN_HEADS * HEAD_DIM
KV_WIDTH = ATTN_KV_HEADS * HEAD_DIM
ROPE_THETA = 10000.0
GDN_HEADS = 4
GDN_DK = 64
GDN_WIDTH = GDN_HEADS * GDN_DK
GDN_CONV = 5
GDN_CHUNK = 64
N_DIR = 2
D_FF = 2816
FFN_CONV = 3
D_IN = POOL_WIDTH + ATTN_WIDTH + 2 * KV_WIDTH + 4 * GDN_WIDTH + 2 * N_DIR * GDN_HEADS

LANES = 128
SUBLANES = 8
HALO = SUBLANES
VMEM_LIMIT = 56 * 1024 * 1024

N_CAT = 2176
COL_POOL = 0
COL_QA = COL_POOL + POOL_WIDTH
COL_KA = COL_QA + ATTN_WIDTH
COL_VA = COL_KA + KV_WIDTH
COL_GQKV = COL_VA + KV_WIDTH
COL_Z = COL_GQKV + 3 * GDN_WIDTH
COL_GATES = COL_Z + GDN_WIDTH

Q_SCALE = HEAD_DIM ** -0.5 * math.log2(math.e)
V_ROWS = HEAD_DIM + 16


def _dot(a, b):
    return jnp.dot(a, b, preferred_element_type=F32)


def _dot_nt(a, b):
    return lax.dot_general(a, b, (((1,), (1,)), ((), ())), preferred_element_type=F32)


def _split3(x):
    x1 = x.astype(BF16)
    r = x - x1.astype(F32)
    x2 = r.astype(BF16)
    x3 = (r - x2.astype(F32)).astype(BF16)
    return x1, x2, x3


def _sel_dot(m01, x):
    x1, x2, x3 = _split3(x)
    return _dot(m01, x1) + _dot(m01, x2) + _dot(m01, x3)


def _dot_sel(x, m01):
    x1, x2, x3 = _split3(x)
    return _dot(x1, m01) + _dot(x2, m01) + _dot(x3, m01)


def _group_sum(x, bd):
    hi = x.astype(BF16)
    lo = (x - hi.astype(F32)).astype(BF16)
    return _dot(hi, bd) + _dot(lo, bd)


def _rms(x, w):
    ms = jnp.mean(x * x, axis=-1, keepdims=True)
    return x * lax.rsqrt(ms + EPS) * w


def _shift_rows(x, d):
    n = x.shape[0]
    return pltpu.roll(x, (-d) % n, 0)


def _params(sem):
    return pltpu.CompilerParams(dimension_semantics=sem, vmem_limit_bytes=VMEM_LIMIT)


def _halo_specs(tm, width, n_rows):
    per = tm // HALO
    last = n_rows // HALO - 1
    return [
        pl.BlockSpec((HALO, width), lambda i: (jnp.maximum(i * per - 1, 0), 0)),
        pl.BlockSpec((tm, width), lambda i: (i, 0)),
        pl.BlockSpec((HALO, width), lambda i: (jnp.minimum((i + 1) * per, last), 0)),
    ]


def _seq_edges(tiles_per_seq):
    t = pl.program_id(0) % tiles_per_seq
    return t, t == 0, t == tiles_per_seq - 1


def _inproj_kernel(x_ref, nw_ref, w_ref, cos_ref, sin_ref, qw_ref, kw_ref, bd_ref,
                   up_ref, q_ref, k_ref, vt_ref, g_ref, z_ref, gt_ref):
    tm = x_ref.shape[0]
    xn = _rms(x_ref[...], nw_ref[...]).astype(BF16)
    up_ref[...] = _dot(xn, w_ref[:, COL_POOL:COL_QA])

    cos = cos_ref[...]
    sin = sin_ref[...]
    lane = lax.broadcasted_iota(jnp.int32, (tm, LANES), 1)
    first_half = (lane & 16) == 0
    lo64 = lane < HEAD_DIM
    bd = bd_ref[...]

    def head_norm_rope(x, w):
        ss = _group_sum(x * x, bd)
        y = x * lax.rsqrt(ss * (1.0 / HEAD_DIM) + EPS) * w
        partner = jnp.where(first_half, pltpu.roll(y, LANES - 16, 1), pltpu.roll(y, 16, 1))
        return y * cos + partner * sin

    qa = _dot(xn, w_ref[:, COL_QA:COL_KA])
    qw = qw_ref[...] * Q_SCALE
    for s in range(ATTN_WIDTH // LANES):
        sl = slice(s * LANES, (s + 1) * LANES)
        q_ref[:, sl] = head_norm_rope(qa[:, sl], qw).astype(BF16)

    ka = _dot(xn, w_ref[:, COL_KA:COL_VA])
    kr = head_norm_rope(ka, kw_ref[...])
    ks = pltpu.roll(kr, HEAD_DIM, 1)
    k_ref[:, 0:128] = jnp.where(lo64, kr, 0.0).astype(BF16)
    k_ref[:, 128:256] = jnp.where(lo64, 0.0, ks).astype(BF16)
    k_ref[:, 256:384] = jnp.where(lo64, ks, 0.0).astype(BF16)
    k_ref[:, 384:512] = jnp.where(lo64, 0.0, kr).astype(BF16)

    vat = _dot(xn, w_ref[:, COL_VA:COL_GQKV]).T
    pad_row = lax.broadcasted_iota(jnp.int32, (V_ROWS - HEAD_DIM, tm), 0)
    ones_rows = jnp.where(pad_row == 0, 1.0, 0.0)
    vt_ref[...] = jnp.concatenate(
        [vat[0:HEAD_DIM], ones_rows, vat[HEAD_DIM:2 * HEAD_DIM], ones_rows], axis=0).astype(BF16)
    g_ref[...] = _dot(xn, w_ref[:, COL_GQKV:COL_Z])
    z_ref[...] = _dot(xn, w_ref[:, COL_Z:COL_GATES])
    gt_ref[...] = _dot(xn, w_ref[:, COL_GATES:N_CAT])


def _inproj(x2d, norm_w, w_cat, cos_t, sin_t, qw, kw, bd128, seq, tm):
    n_tok = x2d.shape[0]
    tps = seq // tm
    const = lambda i: (0, 0)
    row = lambda i: (i, 0)
    pos = lambda i: (i % tps, 0)
    return pl.pallas_call(
        _inproj_kernel,
        grid=(n_tok // tm,),
        in_specs=[
            pl.BlockSpec((tm, D_MODEL), row),
            pl.BlockSpec((1, D_MODEL), const),
            pl.BlockSpec((D_MODEL, N_CAT), const),
            pl.BlockSpec((tm, LANES), pos),
            pl.BlockSpec((tm, LANES), pos),
            pl.BlockSpec((1, LANES), const),
            pl.BlockSpec((1, LANES), const),
            pl.BlockSpec((LANES, LANES), const),
        ],
        out_specs=[
            pl.BlockSpec((tm, POOL_WIDTH), row),
            pl.BlockSpec((tm, ATTN_WIDTH), row),
            pl.BlockSpec((tm, 4 * LANES), row),
            pl.BlockSpec((ATTN_KV_HEADS * V_ROWS, tm), lambda i: (0, i)),
            pl.BlockSpec((tm, 3 * GDN_WIDTH), row),
            pl.BlockSpec((tm, GDN_WIDTH), row),
            pl.BlockSpec((tm, LANES), row),
        ],
        out_shape=[
            jax.ShapeDtypeStruct((n_tok, POOL_WIDTH), F32),
            jax.ShapeDtypeStruct((n_tok, ATTN_WIDTH), BF16),
            jax.ShapeDtypeStruct((n_tok, 4 * LANES), BF16),
            jax.ShapeDtypeStruct((ATTN_KV_HEADS * V_ROWS, n_tok), BF16),
            jax.ShapeDtypeStruct((n_tok, 3 * GDN_WIDTH), F32),
            jax.ShapeDtypeStruct((n_tok, GDN_WIDTH), F32),
            jax.ShapeDtypeStruct((n_tok, LANES), F32),
        ],
        compiler_params=_params(("parallel",)),
        name="inproj",
    )(x2d, norm_w, w_cat, cos_t, sin_t, qw, kw, bd128)


def _pool_kernel(prev_ref, u_ref, next_ref, bdw_ref, sc_ref, o_ref, *, tps, seq):
    tm = u_ref.shape[0]
    t, first, last = _seq_edges(tps)
    u = u_ref[...]
    prev = jnp.where(first, 0.0, prev_ref[...])
    nxt = jnp.where(last, 0.0, next_ref[...])
    ext = jnp.concatenate([prev, u, nxt], axis=0)
    w2 = ext + _shift_rows(ext, -1)
    w4 = _shift_rows(w2, -1) + _shift_rows(w2, 1)
    w8 = _shift_rows(w4, -2) + _shift_rows(w4, 2)
    w16 = _shift_rows(w8, -4) + _shift_rows(w8, 4)
    sl = slice(HALO, HALO + tm)
    grp = lax.broadcasted_iota(jnp.int32, (tm, POOL_WIDTH), 1) // POOL_GDIM
    seg = jnp.where(grp == 0, w2[sl], jnp.where(grp == 1, w4[sl], jnp.where(grp == 2, w8[sl], w16[sl])))
    half = jnp.where(grp == 0, 1, jnp.where(grp == 1, 2, jnp.where(grp == 2, 4, 8)))
    pos = t * tm + lax.broadcasted_iota(jnp.int32, (tm, POOL_WIDTH), 0)
    cnt = (jnp.minimum(pos + half, seq) - jnp.maximum(pos - half, 0)).astype(F32)
    pooled = seg / cnt - u
    o_ref[...] = (_dot(pooled.astype(BF16), bdw_ref[...]) * sc_ref[...]).astype(BF16)


def _pool(u, bdw, scale, seq, tm):
    n_tok = u.shape[0]
    const = lambda i: (0, 0)
    return pl.pallas_call(
        functools.partial(_pool_kernel, tps=seq // tm, seq=seq),
        grid=(n_tok // tm,),
        in_specs=_halo_specs(tm, POOL_WIDTH, n_tok) + [
            pl.BlockSpec((POOL_WIDTH, POOL_WIDTH), const),
            pl.BlockSpec((1, POOL_WIDTH), const),
        ],
        out_specs=pl.BlockSpec((tm, POOL_WIDTH), lambda i: (i, 0)),
        out_shape=jax.ShapeDtypeStruct((n_tok, POOL_WIDTH), BF16),
        compiler_params=_params(("parallel",)),
        name="pool",
    )(u, u, u, bdw, scale)


def _attn_kernel(q_ref, k_ref, vt_ref, o_ref, acc_ref, m_ref, s_ref, bm_ref, *, tk, per_iter):
    seq = k_ref.shape[0]
    group = ATTN_HEADS // ATTN_KV_HEADS
    n_chunks = seq // tk
    acc_ref[...] = jnp.zeros_like(acc_ref)
    m_ref[...] = jnp.full_like(m_ref, -1e30)

    def scores(c, buf, slab):
        r0 = pl.multiple_of(c * tk, tk)
        kk = jnp.concatenate([k_ref[pl.ds(r0, tk), 0:LANES], k_ref[pl.ds(r0, tk), LANES:2 * LANES]], axis=0)
        s2 = _dot_nt(kk, q_ref[:, slab * LANES:(slab + 1) * LANES])
        for half in range(2):
            j = 2 * slab + half
            s = s2[half * tk:(half + 1) * tk]
            s_ref[buf, j] = s
            bm_ref[buf, j] = jnp.max(s, axis=0, keepdims=True)

    def softmax_pv(c, buf, slab):
        r0 = pl.multiple_of(c * tk, tk)
        vt = vt_ref[:, pl.ds(r0, tk)]
        for j in (2 * slab, 2 * slab + 1):
            m = m_ref[j]
            m_new = jnp.maximum(m, bm_ref[buf, j])
            alpha = jnp.exp2(m - m_new)
            p = jnp.exp2(s_ref[buf, j] - m_new)
            m_ref[j] = m_new
            acc_ref[j] = alpha * acc_ref[j] + _dot(vt, p.astype(BF16))

    slabs = range(group // 2)
    for slab in slabs:
        scores(0, 0, slab)

    def body(i, carry):
        c0 = per_iter * i
        for u in range(per_iter):
            for slab in slabs:
                scores(jnp.minimum(c0 + u + 1, n_chunks - 1), (u + 1) % 2, slab)
            for slab in slabs:
                softmax_pv(c0 + u, u % 2, slab)
        return carry

    lax.fori_loop(0, n_chunks // per_iter, body, 0)
    out = jnp.concatenate([acc_ref[j, 0:HEAD_DIM] / acc_ref[j, HEAD_DIM:HEAD_DIM + 1] for j in range(group)], axis=0)
    o_ref[...] = out.T.astype(BF16)


def _attention(q, k4, vt, batch, seq, tq, tk, per_iter):
    n_tok = q.shape[0]
    nq = seq // tq
    group = ATTN_HEADS // ATTN_KV_HEADS
    gw = group * HEAD_DIM
    assert per_iter % 2 == 0 and (seq // tk) % per_iter == 0
    return pl.pallas_call(
        functools.partial(_attn_kernel, tk=tk, per_iter=per_iter),
        grid=(batch, ATTN_KV_HEADS, nq),
        in_specs=[
            pl.BlockSpec((tq, gw), lambda b, g, i: (b * nq + i, g)),
            pl.BlockSpec((seq, 2 * LANES), lambda b, g, i: (b, g)),
            pl.BlockSpec((V_ROWS, seq), lambda b, g, i: (g, b)),
        ],
        out_specs=pl.BlockSpec((tq, gw), lambda b, g, i: (b * nq + i, g)),
        out_shape=jax.ShapeDtypeStruct((n_tok, ATTN_WIDTH), BF16),
        scratch_shapes=[pltpu.VMEM((group, V_ROWS, tq), F32),
                        pltpu.VMEM((group, 1, tq), F32),
                        pltpu.VMEM((2, group, tk, tq), F32),
                        pltpu.VMEM((2, group, 1, tq), F32)],
        compiler_params=_params(("parallel", "parallel", "parallel")),
        name="attention",
    )(q, k4, vt)


def _gdn_prep_kernel(prev_ref, x_ref, next_ref, gt_ref, cw_ref, na_ref, dtb_ref, bd_ref, ex_ref,
                     q_ref, k_ref, v_ref, be_ref, gce_ref, gcr_ref, kgt_ref, *, tps):
    tm = x_ref.shape[0]
    _, first, last = _seq_edges(tps)
    prev = jnp.where(first, 0.0, prev_ref[...])
    nxt = jnp.where(last, 0.0, next_ref[...])
    ext = jnp.concatenate([prev, x_ref[...], nxt], axis=0)
    cw = cw_ref[...]
    conv = ext * cw[2:3]
    for kk in (0, 1, 3, 4):
        conv = conv + _shift_rows(ext, kk - GDN_CONV // 2) * cw[kk:kk + 1]
    conv = conv[HALO:HALO + tm]
    act = conv / (1.0 + jnp.exp(-conv))
    bd = bd_ref[...]

    def l2n(x):
        return x * lax.rsqrt(_group_sum(x * x, bd) + EPS)

    q = l2n(act[:, 0:GDN_WIDTH]) * (GDN_DK ** -0.5)
    k = l2n(act[:, GDN_WIDTH:2 * GDN_WIDTH])
    q_ref[...] = q
    k_ref[...] = k
    v_ref[...] = act[:, 2 * GDN_WIDTH:3 * GDN_WIDTH]

    gt = gt_ref[...]
    lane = lax.broadcasted_iota(jnp.int32, (tm, LANES), 1)
    beta = 1.0 / (1.0 + jnp.exp(-gt))
    xs = gt + dtb_ref[...]
    softplus = jnp.maximum(xs, 0.0) + jnp.log1p(jnp.exp(-jnp.abs(xs)))
    g = na_ref[...] * softplus

    r = lax.broadcasted_iota(jnp.int32, (tm, tm), 0)
    c = lax.broadcasted_iota(jnp.int32, (tm, tm), 1)
    same = (r // GDN_CHUNK) == (c // GDN_CHUNK)
    ones_bd = jnp.where(same, 1.0, 0.0).astype(BF16)
    tri_f = jnp.where(same & (c <= r), 1.0, 0.0).astype(BF16)
    tri_b = jnp.where(same & (c >= r), 1.0, 0.0).astype(BF16)
    gc = jnp.where(lane < 12, _sel_dot(tri_f, g), _sel_dot(tri_b, g))
    comp = jnp.where(lane < 8, beta, gc)
    exp_all = _dot_sel(comp, ex_ref[...])
    tot = _sel_dot(ones_bd, g)
    tot_e = _dot_sel(tot, ex_ref[:, 2 * GDN_WIDTH:4 * GDN_WIDTH])

    row = lax.broadcasted_iota(jnp.int32, (tm, GDN_WIDTH), 0)
    col = lax.broadcasted_iota(jnp.int32, (tm, GDN_WIDTH), 1)
    diag = (row % GDN_CHUNK) == (col % GDN_DK)
    for d in range(N_DIR):
        be_ref[d] = exp_all[:, d * GDN_WIDTH:(d + 1) * GDN_WIDTH]
        gce = exp_all[:, (2 + d) * GDN_WIDTH:(3 + d) * GDN_WIDTH]
        gce_ref[d] = gce
        gcr_ref[d] = _sel_dot(ones_bd, jnp.where(diag, gce, 0.0))
        kg = k * jnp.exp(tot_e[:, d * GDN_WIDTH:(d + 1) * GDN_WIDTH] - gce)
        kgt_ref[d] = kg.T


def _gdn_prep(gqkv, gates, conv_w, neg_a, dt_b, bd256, expand, seq, tm):
    n_tok = gqkv.shape[0]
    width = 3 * GDN_WIDTH
    const = lambda i: (0, 0)
    row = lambda i: (i, 0)
    drow = lambda i: (0, i, 0)
    tok = jax.ShapeDtypeStruct((n_tok, GDN_WIDTH), F32)
    dtok = jax.ShapeDtypeStruct((N_DIR, n_tok, GDN_WIDTH), F32)
    return pl.pallas_call(
        functools.partial(_gdn_prep_kernel, tps=seq // tm),
        grid=(n_tok // tm,),
        in_specs=_halo_specs(tm, width, n_tok) + [
            pl.BlockSpec((tm, LANES), row),
            pl.BlockSpec((GDN_CONV, width), const),
            pl.BlockSpec((1, LANES), const),
            pl.BlockSpec((1, LANES), const),
            pl.BlockSpec((GDN_WIDTH, GDN_WIDTH), const),
            pl.BlockSpec((LANES, 4 * GDN_WIDTH), const),
        ],
        out_specs=[
            pl.BlockSpec((tm, GDN_WIDTH), row),
            pl.BlockSpec((tm, GDN_WIDTH), row),
            pl.BlockSpec((tm, GDN_WIDTH), row),
            pl.BlockSpec((N_DIR, tm, GDN_WIDTH), drow),
            pl.BlockSpec((N_DIR, tm, GDN_WIDTH), drow),
            pl.BlockSpec((N_DIR, tm, GDN_WIDTH), drow),
            pl.BlockSpec((N_DIR, GDN_WIDTH, tm), lambda i: (0, 0, i)),
        ],
        out_shape=[tok, tok, tok, dtok, dtok, dtok,
                   jax.ShapeDtypeStruct((N_DIR, GDN_WIDTH, n_tok), F32)],
        compiler_params=_params(("parallel",)),
        name="gdn_prep",
    )(gqkv, gqkv, gqkv, gates, conv_w, neg_a, dt_b, bd256, expand)


def _gdn_scan_kernel(qf_ref, kf_ref, vf_ref, bef_ref, gcef_ref, gcrf_ref,
                     qb_ref, kb_ref, vb_ref, beb_ref, gceb_ref, gcrb_ref, kgtf_ref, kgtb_ref,
                     of_ref, ob_ref, state_ref, stage_ref, gl_ref, *, chunks):
    cs = GDN_CHUNK
    w = GDN_WIDTH
    par = pl.program_id(1) % 2
    prv = 1 - par

    @pl.when(pl.program_id(1) == 0)
    def _():
        state_ref[...] = jnp.zeros_like(state_ref)
        stage_ref[...] = jnp.zeros_like(stage_ref)
        gl_ref[...] = jnp.zeros_like(gl_ref)

    lane_h = lax.broadcasted_iota(jnp.int32, (cs, w), 1) // GDN_DK
    heads = [lane_h == h for h in range(GDN_HEADS)]
    ti = lax.broadcasted_iota(jnp.int32, (cs, w), 0)
    tj = lax.broadcasted_iota(jnp.int32, (cs, w), 1) % cs
    eye = (ti == tj).astype(F32)
    sr = lax.broadcasted_iota(jnp.int32, (w, w), 0) // GDN_DK
    sc = lax.broadcasted_iota(jnp.int32, (w, w), 1) // GDN_DK
    state_mask = sr == sc
    zeros = jnp.zeros((cs, w), F32)
    in_refs = ((qf_ref, kf_ref, vf_ref, bef_ref, gcef_ref, gcrf_ref),
               (qb_ref, kb_ref, vb_ref, beb_ref, gceb_ref, gcrb_ref))
    kgt_refs = (kgtf_ref, kgtb_ref)
    o_refs = (of_ref, ob_ref)
    U, W, QG, QKM = range(4)

    def blockdiag(y):
        return jnp.concatenate([jnp.where(hm, y, 0.0) for hm in heads], axis=0).astype(BF16)

    def rows_of(ci):
        return slice(ci * cs, (ci + 1) * cs)

    loc = {}

    def local_scores(d, ci):
        q_ref, k_ref, v_ref, be_ref, gce_ref, gcr_ref = in_refs[d]
        rs = rows_of(ci)
        q = q_ref[rs, :]
        k = k_ref[rs, :]
        be = be_ref[0, rs, :]
        kb = k * be
        kq = _dot_nt(jnp.concatenate([kb, q], axis=0).astype(BF16), blockdiag(k))
        loc[d, ci] = dict(kq=kq, kb=kb)

    def local_decay(d, ci):
        st = loc[d, ci]
        _, _, _, _, gce_ref, gcr_ref = in_refs[d]
        rs = rows_of(ci)
        incl = (ti >= tj) if d == 0 else (ti <= tj)
        strict = (ti > tj) if d == 0 else (ti < tj)
        diff = gce_ref[0, rs, :] - gcr_ref[0, rs, :]
        decay = jnp.where(incl, jnp.exp(jnp.where(incl, diff, 0.0)), 0.0)
        kq = st.pop("kq")
        a = jnp.where(strict, kq[0:cs] * decay, 0.0)
        stage_ref[par, d, QKM, rs, :] = kq[cs:2 * cs] * decay
        st["pw"] = _dot(a.astype(BF16), blockdiag(a))
        st["t"] = eye - a

    def local_invert(d, ci, last):
        st = loc[d, ci]
        bdp = blockdiag(st["pw"])
        if last:
            st["t"] = st["t"] + _dot(st["t"].astype(BF16), bdp)
        else:
            both = _dot(jnp.concatenate([st["pw"], st["t"]], axis=0).astype(BF16), bdp)
            st["pw"] = both[0:cs]
            st["t"] = st["t"] + both[cs:2 * cs]

    def local_uw(d, ci):
        st = loc.pop((d, ci))
        q_ref, _, v_ref, be_ref, gce_ref, _ = in_refs[d]
        rs = rows_of(ci)
        eg = jnp.exp(gce_ref[0, rs, :])
        rhs = jnp.concatenate([blockdiag(v_ref[rs, :] * be_ref[0, rs, :]), blockdiag(st["kb"] * eg)], axis=1)
        uw = _dot(st["t"].astype(BF16), rhs)
        stage_ref[par, d, U, rs, :] = uw[:, 0:w]
        stage_ref[par, d, W, rs, :] = uw[:, w:2 * w]
        stage_ref[par, d, QG, rs, :] = q_ref[rs, :] * eg
        edge = (ci + 1) * cs - 1 if d == 0 else ci * cs
        gl_ref[par, d, ci] = jnp.exp(gce_ref[0, edge:edge + 1, :])

    scan = {}

    def scan_read(d, ci):
        rs = rows_of(ci)
        lhs = jnp.concatenate([stage_ref[prv, d, W, rs, :], stage_ref[prv, d, QG, rs, :]], axis=0)
        scan[d] = _dot(lhs.astype(BF16), state_ref[d].astype(BF16))

    def scan_out(d, ci):
        rs = rows_of(ci)
        ws = scan[d]
        vn = stage_ref[prv, d, U, rs, :] - ws[0:cs]
        o_refs[d][rs, :] = ws[cs:2 * cs] + _dot(stage_ref[prv, d, QKM, rs, :].astype(BF16), blockdiag(vn))
        scan[d] = vn

    def scan_update(d, ci):
        vn = scan[d]
        pair = jnp.concatenate([vn, zeros] if ci % 2 == 0 else [zeros, vn], axis=0).astype(BF16)
        kgt = kgt_refs[d][0, :, (ci // 2) * LANES:(ci // 2 + 1) * LANES].astype(BF16)
        upd = jnp.where(state_mask, _dot(kgt, pair), 0.0)
        state_ref[d] = state_ref[d] * gl_ref[prv, d, ci] + upd

    order = [(d, ci if d == 0 else chunks - 1 - ci) for ci in range(chunks) for d in range(N_DIR)]
    local_ops = [functools.partial(local_scores, d, ci) for d, ci in order]
    local_ops += [functools.partial(local_decay, d, ci) for d, ci in order]
    for step in range(5):
        local_ops += [functools.partial(local_invert, d, ci, step == 4) for d, ci in order]
    local_ops += [functools.partial(local_uw, d, ci) for d, ci in order]
    scan_ops = []
    for ci in range(chunks):
        for fn in (scan_read, scan_out, scan_update):
            scan_ops.append([functools.partial(fn, d, ci if d == 0 else chunks - 1 - ci) for d in range(N_DIR)])
    every = len(local_ops) // len(scan_ops)
    for idx, op in enumerate(local_ops):
        op()
        if (idx + 1) % every == 0 and scan_ops:
            for s_op in scan_ops.pop(0):
                s_op()
    for level in scan_ops:
        for s_op in level:
            s_op()


def _gdn_scan(q, k, v, be, gce, gcr, kgt, batch, seq, chunks):
    n_tok = q.shape[0]
    rows = chunks * GDN_CHUNK
    ng = seq // rows
    w = GDN_WIDTH

    def group(d, lag):
        def f(b, n):
            g = jnp.clip(n - lag, 0, ng - 1)
            return b * ng + (ng - 1 - g if d == 1 else g)
        return f

    def tok(d, lag):
        f = group(d, lag)
        return lambda b, n: (f(b, n), 0)

    def dtok(d, lag):
        f = group(d, lag)
        return lambda b, n: (d, f(b, n), 0)

    def dlane(d, lag):
        f = group(d, lag)
        return lambda b, n: (d, 0, f(b, n))

    def local_specs(d):
        return [pl.BlockSpec((rows, w), tok(d, 0))] * 3 + [pl.BlockSpec((1, rows, w), dtok(d, 0))] * 3

    out = jax.ShapeDtypeStruct((n_tok, w), F32)
    return pl.pallas_call(
        functools.partial(_gdn_scan_kernel, chunks=chunks),
        grid=(batch, ng + 1),
        in_specs=local_specs(0) + local_specs(1) + [pl.BlockSpec((1, w, rows), dlane(d, 1)) for d in range(N_DIR)],
        out_specs=[pl.BlockSpec((rows, w), tok(d, 1)) for d in range(N_DIR)],
        out_shape=[out, out],
        scratch_shapes=[pltpu.VMEM((N_DIR, w, w), F32),
                        pltpu.VMEM((2, N_DIR, 4, rows, w), F32),
                        pltpu.VMEM((2, N_DIR, chunks, 1, w), F32)],
        compiler_params=_params(("parallel", "arbitrary")),
        name="gdn_scan",
    )(q, k, v, be, gce, gcr, q, k, v, be, gce, gcr, kgt, kgt)


def _outproj_kernel(yp_ref, ya_ref, of_ref, ob_ref, z_ref, h_ref, gw_ref, bd_ref,
                    wp_ref, wa_ref, wg_ref, nw_ref, o_ref):
    o = of_ref[...] + ob_ref[...]
    ms = _group_sum(o * o, bd_ref[...]) * (1.0 / GDN_DK)
    z = z_ref[...]
    yg = o * lax.rsqrt(ms + EPS) * gw_ref[...] * (z / (1.0 + jnp.exp(-z)))
    mix = _dot(yp_ref[...], wp_ref[...]) + _dot(ya_ref[...], wa_ref[...]) + _dot(yg.astype(BF16), wg_ref[...])
    o_ref[...] = h_ref[...] + _rms(mix, nw_ref[...])


def _outproj(y_pool, y_attn, o_f, o_b, z, h, gdn_nw, bd256, w_out, post_w, tm):
    n_tok = h.shape[0]
    const = lambda i: (0, 0)
    row = lambda i: (i, 0)
    a0 = POOL_WIDTH
    a1 = POOL_WIDTH + ATTN_WIDTH
    return pl.pallas_call(
        _outproj_kernel,
        grid=(n_tok // tm,),
        in_specs=[
            pl.BlockSpec((tm, POOL_WIDTH), row),
            pl.BlockSpec((tm, ATTN_WIDTH), row),
            pl.BlockSpec((tm, GDN_WIDTH), row),
            pl.BlockSpec((tm, GDN_WIDTH), row),
            pl.BlockSpec((tm, GDN_WIDTH), row),
            pl.BlockSpec((tm, D_MODEL), row),
            pl.BlockSpec((1, GDN_WIDTH), const),
            pl.BlockSpec((GDN_WIDTH, GDN_WIDTH), const),
            pl.BlockSpec((POOL_WIDTH, D_MODEL), const),
            pl.BlockSpec((ATTN_WIDTH, D_MODEL), const),
            pl.BlockSpec((GDN_WIDTH, D_MODEL), const),
            pl.BlockSpec((1, D_MODEL), const),
        ],
        out_specs=pl.BlockSpec((tm, D_MODEL), row),
        out_shape=jax.ShapeDtypeStruct((n_tok, D_MODEL), F32),
        compiler_params=_params(("parallel",)),
        name="outproj",
    )(y_pool, y_attn, o_f, o_b, z, h, gdn_nw, bd256, w_out[:a0], w_out[a0:a1], w_out[a1:], post_w)


def _ffn_kernel(prev_ref, h_ref, next_ref, prew_ref, wg_ref, wv_ref, cg_ref, cv_ref, wd_ref, postw_ref,
                o_ref, act_ref, *, tps, chunk):
    tm = h_ref.shape[0]
    _, first, last = _seq_edges(tps)
    h = h_ref[...]
    prew = prew_ref[...]
    prev = jnp.where(first, 0.0, _rms(prev_ref[...], prew))
    nxt = jnp.where(last, 0.0, _rms(next_ref[...], prew))
    xn = jnp.concatenate([prev, _rms(h, prew), nxt], axis=0).astype(BF16)
    sl = slice(HALO, HALO + tm)

    def conv3(up, cw):
        out = _shift_rows(up, -1) * cw[0:1] + up * cw[1:2] + _shift_rows(up, 1) * cw[2:3]
        return out[sl]

    for c in range(D_FF // chunk):
        cs = slice(c * chunk, (c + 1) * chunk)
        gate = conv3(_dot(xn, wg_ref[:, cs]), cg_ref[:, cs])
        val = conv3(_dot(xn, wv_ref[:, cs]), cv_ref[:, cs])
        inner = 0.7978845608028654 * (gate + 0.044715 * (gate * gate * gate))
        act_ref[:, cs] = (0.5 * gate * (1.0 + jnp.tanh(inner)) * val).astype(BF16)
    ff = _dot(act_ref[...], wd_ref[...])
    o_ref[...] = h + _rms(ff, postw_ref[...])


def _ffn(h, pre_w, w_gate, w_val, c_gate, c_val, w_down, post_w, seq, tm, chunk):
    n_tok = h.shape[0]
    const = lambda i: (0, 0)
    return pl.pallas_call(
        functools.partial(_ffn_kernel, tps=seq // tm, chunk=chunk),
        grid=(n_tok // tm,),
        in_specs=_halo_specs(tm, D_MODEL, n_tok) + [
            pl.BlockSpec((1, D_MODEL), const),
            pl.BlockSpec((D_MODEL, D_FF), const),
            pl.BlockSpec((D_MODEL, D_FF), const),
            pl.BlockSpec((FFN_CONV, D_FF), const),
            pl.BlockSpec((FFN_CONV, D_FF), const),
            pl.BlockSpec((D_FF, D_MODEL), const),
            pl.BlockSpec((1, D_MODEL), const),
        ],
        out_specs=pl.BlockSpec((tm, D_MODEL), lambda i: (i, 0)),
        out_shape=jax.ShapeDtypeStruct((n_tok, D_MODEL), F32),
        scratch_shapes=[pltpu.VMEM((tm, D_FF), BF16)],
        compiler_params=_params(("parallel",)),
        name="ffn",
    )(h, h, h, pre_w, w_gate, w_val, c_gate, c_val, w_down, post_w)


def _rope_tables(seq):
    t = np.arange(seq)
    pos = np.stack([t // GRID_W, t % GRID_W], axis=-1).astype(np.float32)
    axis_dim = HEAD_DIM // 2
    inv_freq = (ROPE_THETA ** (-np.arange(0, axis_dim, 2, dtype=np.float32) / axis_dim)).astype(np.float32)
    ang = pos[:, :, None] * inv_freq
    j = np.arange(LANES) % HEAD_DIM
    a = ang[:, j // 32, j % 16]
    sign = np.where((j % 32) < 16, -1.0, 1.0).astype(np.float32)
    return jnp.asarray(np.cos(a), F32), jnp.asarray(np.sin(a) * sign, F32)


def _head_blockdiag(n):
    i = np.arange(n) // HEAD_DIM
    return jnp.asarray(i[:, None] == i[None, :], BF16)


def _gate_expand():
    e = np.zeros((LANES, 4 * GDN_WIDTH), np.float32)
    for kind in range(2):
        for d in range(N_DIR):
            for hh in range(GDN_HEADS):
                src = kind * N_DIR * GDN_HEADS + d * GDN_HEADS + hh
                dst = (kind * N_DIR + d) * GDN_WIDTH + hh * GDN_DK
                e[src, dst:dst + GDN_DK] = 1.0
    return jnp.asarray(e, BF16)


def _pad_lanes(v, offset):
    out = jnp.zeros((1, LANES), F32)
    return out.at[0, offset:offset + v.shape[0]].set(v.astype(F32))


def kernel(x, pre_mix_norm, post_mix_norm, pre_ffn_norm, post_ffn_norm, w_in, pool_w, pool_scale, q_norm, k_norm, gdn_conv, gdn_a_log, gdn_dt_bias, gdn_norm, w_out, w_up, ffn_conv, w_down):
    batch, seq, _ = x.shape
    n_tok = batch * seq
    depth = w_in.shape[0]
    assert seq % 512 == 0 and seq % GRID_W == 0

    tm_proj = 512
    tm_pool = 512
    tm_gdn = 256
    tm_ffn = 512
    cos_t, sin_t = _rope_tables(seq)
    bd128 = _head_blockdiag(LANES)
    bd256 = _head_blockdiag(GDN_WIDTH)
    expand = _gate_expand()
    n_gate = N_DIR * GDN_HEADS

    h = x.reshape(n_tok, D_MODEL)
    for l in range(depth):
        w_cat = jnp.pad(w_in[l], ((0, 0), (0, N_CAT - D_IN))).astype(BF16)
        qw = jnp.tile(q_norm[l], 2)[None, :]
        kw = jnp.tile(k_norm[l], 2)[None, :]
        u_pool, q, k4, vt, gqkv, z, gates = _inproj(
            h, pre_mix_norm[l][None, :], w_cat, cos_t, sin_t, qw, kw, bd128, seq, tm_proj)

        bdw = jax.scipy.linalg.block_diag(*[pool_w[l, g] for g in range(POOL_GROUPS)]).astype(BF16)
        y_pool = _pool(u_pool, bdw, pool_scale[l][None, :], seq, tm_pool)

        y_attn = _attention(q, k4, vt, batch, seq, 256, 256, 4)

        neg_a = _pad_lanes(-jnp.exp(gdn_a_log[l].reshape(-1)), n_gate)
        dt_b = _pad_lanes(gdn_dt_bias[l].reshape(-1), n_gate)
        gq, gk, gv, be, gce, gcr, kgt = _gdn_prep(gqkv, gates, gdn_conv[l], neg_a, dt_b, bd256, expand, seq, tm_gdn)
        o_f, o_b = _gdn_scan(gq, gk, gv, be, gce, gcr, kgt, batch, seq, 4)

        h = _outproj(y_pool, y_attn, o_f, o_b, z, h, jnp.tile(gdn_norm[l], GDN_HEADS)[None, :], bd256,
                     w_out[l].astype(BF16), post_mix_norm[l][None, :], 512)

        wu = w_up[l].astype(BF16)
        h = _ffn(h, pre_ffn_norm[l][None, :], wu[:, :D_FF], wu[:, D_FF:], ffn_conv[l][:, :D_FF], ffn_conv[l][:, D_FF:],
                 w_down[l].astype(BF16), post_ffn_norm[l][None, :], seq, tm_ffn, 256)
    return h.reshape(batch, seq, D_MODEL)
```

```python
import functools
import math

import numpy as np
import jax
import jax.numpy as jnp
from jax import lax
from jax.experimental import pallas as pl
from jax.experimental.pallas import tpu as pltpu

F32 = jnp.float32
BF16 = jnp.bfloat16

D_MODEL = 1024
GRID_W = 64
EPS = 1e-6
POOL_WIDTH = 256
POOL_WINDOWS = (2, 4, 8, 16)
POOL_GROUPS = 4
POOL_GDIM = POOL_WIDTH // POOL_GROUPS
ATTN_HEADS = 8
ATTN_KV_HEADS = 2
HEAD_DIM = 64
ATTN_WIDTH = ATTN_HEADS * HEAD_DIM
KV_WIDTH = ATTN_KV_HEADS * HEAD_DIM
ROPE_THETA = 10000.0
GDN_HEADS = 4
GDN_DK = 64
GDN_WIDTH = GDN_HEADS * GDN_DK
GDN_CONV = 5
GDN_CHUNK = 64
N_DIR = 2
D_FF = 2816
FFN_CONV = 3
D_IN = POOL_WIDTH + ATTN_WIDTH + 2 * KV_WIDTH + 4 * GDN_WIDTH + 2 * N_DIR * GDN_HEADS

LANES = 128
SUBLANES = 8
HALO = SUBLANES
VMEM_LIMIT = 56 * 1024 * 1024

N_CAT = 2176
COL_POOL = 0
COL_QA = COL_POOL + POOL_WIDTH
COL_KA = COL_QA + ATTN_WIDTH
COL_VA = COL_KA + KV_WIDTH
COL_GQKV = COL_VA + KV_WIDTH
COL_Z = COL_GQKV + 3 * GDN_WIDTH
COL_GATES = COL_Z + GDN_WIDTH

Q_SCALE = HEAD_DIM ** -0.5 * math.log2(math.e)
V_ROWS = HEAD_DIM + 16


def _dot(a, b):
    return jnp.dot(a, b, preferred_element_type=F32)


def _dot_nt(a, b):
    return lax.dot_general(a, b, (((1,), (1,)), ((), ())), preferred_element_type=F32)


def _split2(x):
    hi = x.astype(BF16)
    lo = (x - hi.astype(F32)).astype(BF16)
    return hi, lo


def _sel_dot(m01, x):
    hi, lo = _split2(x)
    return _dot(m01, hi) + _dot(m01, lo)


def _dot_sel(x, m01):
    hi, lo = _split2(x)
    return _dot(hi, m01) + _dot(lo, m01)


def _group_sum(x, bd):
    return _dot_sel(x, bd)


def _rms(x, w):
    ms = jnp.mean(x * x, axis=-1, keepdims=True)
    return x * lax.rsqrt(ms + EPS) * w


def _shift_rows(x, d):
    n = x.shape[0]
    return pltpu.roll(x, (-d) % n, 0)


def _params(sem):
    return pltpu.CompilerParams(dimension_semantics=sem, vmem_limit_bytes=VMEM_LIMIT)


def _halo_specs(tm, width, n_rows):
    per = tm // HALO
    last = n_rows // HALO - 1
    return [
        pl.BlockSpec((HALO, width), lambda i: (jnp.maximum(i * per - 1, 0), 0)),
        pl.BlockSpec((tm, width), lambda i: (i, 0)),
        pl.BlockSpec((HALO, width), lambda i: (jnp.minimum((i + 1) * per, last), 0)),
    ]


def _seq_edges(tiles_per_seq):
    t = pl.program_id(0) % tiles_per_seq
    return t, t == 0, t == tiles_per_seq - 1


def _inproj_kernel(x_ref, nw_ref, w_ref, cos_ref, sin_ref, qw_ref, kw_ref, bd_ref,
                   up_ref, q_ref, k_ref, vt_ref, g_ref, z_ref, gt_ref):
    tm = x_ref.shape[0]
    xn = _rms(x_ref[...], nw_ref[...]).astype(BF16)

    cos = cos_ref[...]
    sin = sin_ref[...]
    lane = lax.broadcasted_iota(jnp.int32, (tm, LANES), 1)
    first_half = (lane & 16) == 0
    lo64 = lane < HEAD_DIM
    bd = bd_ref[...]

    def head_norm_rope(x, ss, w):
        y = x * lax.rsqrt(ss * (1.0 / HEAD_DIM) + EPS) * w
        partner = jnp.where(first_half, pltpu.roll(y, LANES - 16, 1), pltpu.roll(y, 16, 1))
        return y * cos + partner * sin

    n_slab = ATTN_WIDTH // LANES
    qa = _dot(xn, w_ref[:, COL_QA:COL_KA])
    ka = _dot(xn, w_ref[:, COL_KA:COL_VA])
    ss_q = [_group_sum(qa[:, p * 2 * LANES:(p + 1) * 2 * LANES] ** 2, bd) for p in range(n_slab // 2)]
    ss_q = [ss_q[s // 2][:, (s % 2) * LANES:(s % 2 + 1) * LANES] for s in range(n_slab)]
    qa = [qa[:, s * LANES:(s + 1) * LANES] for s in range(n_slab)]
    ss_k = _group_sum(ka * ka, bd[0:LANES, 0:LANES])
    up_ref[...] = _dot(xn, w_ref[:, COL_POOL:COL_QA])

    qw = qw_ref[...] * Q_SCALE
    q_rot = jnp.concatenate([head_norm_rope(qa[s], ss_q[s], qw) for s in range(n_slab)], axis=1)
    q_ref[...] = q_rot.T.astype(BF16)

    kr = head_norm_rope(ka, ss_k, kw_ref[...])
    ks = pltpu.roll(kr, HEAD_DIM, 1)
    k_ref[:, 0:128] = jnp.where(lo64, kr, 0.0).astype(BF16)
    k_ref[:, 128:256] = jnp.where(lo64, 0.0, ks).astype(BF16)
    k_ref[:, 256:384] = jnp.where(lo64, ks, 0.0).astype(BF16)
    k_ref[:, 384:512] = jnp.where(lo64, 0.0, kr).astype(BF16)

    vat = _dot(xn, w_ref[:, COL_VA:COL_GQKV]).T
    pad_row = lax.broadcasted_iota(jnp.int32, (V_ROWS - HEAD_DIM, tm), 0)
    ones_rows = jnp.where(pad_row == 0, 1.0, 0.0)
    vt_ref[...] = jnp.concatenate(
        [vat[0:HEAD_DIM], ones_rows, vat[HEAD_DIM:2 * HEAD_DIM], ones_rows], axis=0).astype(BF16)
    g_ref[...] = _dot(xn, w_ref[:, COL_GQKV:COL_Z])
    z_ref[...] = _dot(xn, w_ref[:, COL_Z:COL_GATES])
    gt_ref[...] = _dot(xn, w_ref[:, COL_GATES:N_CAT])


def _inproj(x2d, norm_w, w_cat, cos_t, sin_t, qw, kw, bd256, seq, tm):
    n_tok = x2d.shape[0]
    tps = seq // tm
    const = lambda i: (0, 0)
    row = lambda i: (i, 0)
    pos = lambda i: (i % tps, 0)
    return pl.pallas_call(
        _inproj_kernel,
        grid=(n_tok // tm,),
        in_specs=[
            pl.BlockSpec((tm, D_MODEL), row),
            pl.BlockSpec((1, D_MODEL), const),
            pl.BlockSpec((D_MODEL, N_CAT), const),
            pl.BlockSpec((tm, LANES), pos),
            pl.BlockSpec((tm, LANES), pos),
            pl.BlockSpec((1, LANES), const),
            pl.BlockSpec((1, LANES), const),
            pl.BlockSpec((2 * LANES, 2 * LANES), const),
        ],
        out_specs=[
            pl.BlockSpec((tm, POOL_WIDTH), row),
            pl.BlockSpec((ATTN_WIDTH, tm), lambda i: (0, i)),
            pl.BlockSpec((tm, 4 * LANES), row),
            pl.BlockSpec((ATTN_KV_HEADS * V_ROWS, tm), lambda i: (0, i)),
            pl.BlockSpec((tm, 3 * GDN_WIDTH), row),
            pl.BlockSpec((tm, GDN_WIDTH), row),
            pl.BlockSpec((tm, LANES), row),
        ],
        out_shape=[
            jax.ShapeDtypeStruct((n_tok, POOL_WIDTH), F32),
            jax.ShapeDtypeStruct((ATTN_WIDTH, n_tok), BF16),
            jax.ShapeDtypeStruct((n_tok, 4 * LANES), BF16),
            jax.ShapeDtypeStruct((ATTN_KV_HEADS * V_ROWS, n_tok), BF16),
            jax.ShapeDtypeStruct((n_tok, 3 * GDN_WIDTH), F32),
            jax.ShapeDtypeStruct((n_tok, GDN_WIDTH), F32),
            jax.ShapeDtypeStruct((n_tok, LANES), F32),
        ],
        compiler_params=_params(("parallel",)),
        name="inproj",
    )(x2d, norm_w, w_cat, cos_t, sin_t, qw, kw, bd256)


def _pool_kernel(prev_ref, u_ref, next_ref, bdw_ref, sc_ref, o_ref, *, tps, seq):
    tm = u_ref.shape[0]
    t, first, last = _seq_edges(tps)
    u = u_ref[...]
    prev = jnp.where(first, 0.0, prev_ref[...])
    nxt = jnp.where(last, 0.0, next_ref[...])
    ext = jnp.concatenate([prev, u, nxt], axis=0)
    w2 = ext + _shift_rows(ext, -1)
    w4 = _shift_rows(w2, -1) + _shift_rows(w2, 1)
    w8 = _shift_rows(w4, -2) + _shift_rows(w4, 2)
    w16 = _shift_rows(w8, -4) + _shift_rows(w8, 4)
    sl = slice(HALO, HALO + tm)
    grp = lax.broadcasted_iota(jnp.int32, (tm, POOL_WIDTH), 1) // POOL_GDIM
    seg = jnp.where(grp == 0, w2[sl], jnp.where(grp == 1, w4[sl], jnp.where(grp == 2, w8[sl], w16[sl])))
    half = jnp.where(grp == 0, 1, jnp.where(grp == 1, 2, jnp.where(grp == 2, 4, 8)))
    pos = t * tm + lax.broadcasted_iota(jnp.int32, (tm, POOL_WIDTH), 0)
    cnt = (jnp.minimum(pos + half, seq) - jnp.maximum(pos - half, 0)).astype(F32)
    pooled = seg / cnt - u
    o_ref[...] = (_dot(pooled.astype(BF16), bdw_ref[...]) * sc_ref[...]).astype(BF16)


def _pool(u, bdw, scale, seq, tm):
    n_tok = u.shape[0]
    const = lambda i: (0, 0)
    return pl.pallas_call(
        functools.partial(_pool_kernel, tps=seq // tm, seq=seq),
        grid=(n_tok // tm,),
        in_specs=_halo_specs(tm, POOL_WIDTH, n_tok) + [
            pl.BlockSpec((POOL_WIDTH, POOL_WIDTH), const),
            pl.BlockSpec((1, POOL_WIDTH), const),
        ],
        out_specs=pl.BlockSpec((tm, POOL_WIDTH), lambda i: (i, 0)),
        out_shape=jax.ShapeDtypeStruct((n_tok, POOL_WIDTH), BF16),
        compiler_params=_params(("parallel",)),
        name="pool",
    )(u, u, u, bdw, scale)


def _attn_kernel(q_ref, k_ref, vt_ref, o_ref, acc_ref, m_ref, s_ref, bm_ref, *, tk, per_iter):
    seq = k_ref.shape[0]
    group = ATTN_HEADS // ATTN_KV_HEADS
    n_chunks = seq // tk
    acc_ref[...] = jnp.zeros_like(acc_ref)
    m_ref[...] = jnp.full_like(m_ref, -1e30)

    def scores(c, buf, slab):
        r0 = pl.multiple_of(c * tk, tk)
        kk = jnp.concatenate([k_ref[pl.ds(r0, tk), 0:LANES], k_ref[pl.ds(r0, tk), LANES:2 * LANES]], axis=0)
        s2 = _dot(kk, q_ref[slab * LANES:(slab + 1) * LANES, :])
        for half in range(2):
            j = 2 * slab + half
            s = s2[half * tk:(half + 1) * tk]
            s_ref[buf, j] = s
            bm_ref[buf, j] = jnp.max(s, axis=0, keepdims=True)

    def softmax_pv(c, buf, slab):
        r0 = pl.multiple_of(c * tk, tk)
        vt = vt_ref[:, pl.ds(r0, tk)]
        for j in (2 * slab, 2 * slab + 1):
            m = m_ref[j]
            m_new = jnp.maximum(m, bm_ref[buf, j])
            alpha = jnp.exp2(m - m_new)
            p = jnp.exp2(s_ref[buf, j] - m_new)
            m_ref[j] = m_new
            acc_ref[j] = alpha * acc_ref[j] + _dot(vt, p.astype(BF16))

    slabs = range(group // 2)
    for slab in slabs:
        scores(0, 0, slab)

    def body(i, carry):
        c0 = per_iter * i
        for u in range(per_iter):
            for slab in slabs:
                scores(jnp.minimum(c0 + u + 1, n_chunks - 1), (u + 1) % 2, slab)
            for slab in slabs:
                softmax_pv(c0 + u, u % 2, slab)
        return carry

    lax.fori_loop(0, n_chunks // per_iter, body, 0)
    out = jnp.concatenate([acc_ref[j, 0:HEAD_DIM] / acc_ref[j, HEAD_DIM:HEAD_DIM + 1] for j in range(group)], axis=0)
    o_ref[...] = out.T.astype(BF16)


def _attention(qt, k4, vt, batch, seq, tq, tk, per_iter):
    n_tok = k4.shape[0]
    nq = seq // tq
    group = ATTN_HEADS // ATTN_KV_HEADS
    gw = group * HEAD_DIM
    per_iter = min(per_iter, seq // tk)
    assert per_iter % 2 == 0 and (seq // tk) % per_iter == 0
    return pl.pallas_call(
        functools.partial(_attn_kernel, tk=tk, per_iter=per_iter),
        grid=(batch, ATTN_KV_HEADS, nq),
        in_specs=[
            pl.BlockSpec((gw, tq), lambda b, g, i: (g, b * nq + i)),
            pl.BlockSpec((seq, 2 * LANES), lambda b, g, i: (b, g)),
            pl.BlockSpec((V_ROWS, seq), lambda b, g, i: (g, b)),
        ],
        out_specs=pl.BlockSpec((tq, gw), lambda b, g, i: (b * nq + i, g)),
        out_shape=jax.ShapeDtypeStruct((n_tok, ATTN_WIDTH), BF16),
        scratch_shapes=[pltpu.VMEM((group, V_ROWS, tq), F32),
                        pltpu.VMEM((group, 1, tq), F32),
                        pltpu.VMEM((2, group, tk, tq), F32),
                        pltpu.VMEM((2, group, 1, tq), F32)],
        compiler_params=_params(("parallel", "parallel", "parallel")),
        name="attention",
    )(qt, k4, vt)


def _gdn_prep_kernel(prev_ref, x_ref, next_ref, gt_ref, cw_ref, na_ref, dtb_ref, bd_ref, ex_ref,
                     q_ref, k_ref, v_ref, be_ref, gce_ref, gcr_ref, kgt_ref, *, tps):
    tm = x_ref.shape[0]
    _, first, last = _seq_edges(tps)
    prev = jnp.where(first, 0.0, prev_ref[...])
    nxt = jnp.where(last, 0.0, next_ref[...])
    ext = jnp.concatenate([prev, x_ref[...], nxt], axis=0)
    cw = cw_ref[...]
    conv = ext * cw[2:3]
    for kk in (0, 1, 3, 4):
        conv = conv + _shift_rows(ext, kk - GDN_CONV // 2) * cw[kk:kk + 1]
    conv = conv[HALO:HALO + tm]
    act = conv / (1.0 + jnp.exp(-conv))
    bd = bd_ref[...]

    def l2n(x):
        return x * lax.rsqrt(_group_sum(x * x, bd) + EPS)

    q = l2n(act[:, 0:GDN_WIDTH]) * (GDN_DK ** -0.5)
    k = l2n(act[:, GDN_WIDTH:2 * GDN_WIDTH])
    q_ref[...] = q
    k_ref[...] = k
    v_ref[...] = act[:, 2 * GDN_WIDTH:3 * GDN_WIDTH]

    gt = gt_ref[...]
    lane = lax.broadcasted_iota(jnp.int32, (tm, LANES), 1)
    beta = 1.0 / (1.0 + jnp.exp(-gt))
    xs = gt + dtb_ref[...]
    softplus = jnp.maximum(xs, 0.0) + jnp.log1p(jnp.exp(-jnp.abs(xs)))
    g = na_ref[...] * softplus

    r = lax.broadcasted_iota(jnp.int32, (tm, tm), 0)
    c = lax.broadcasted_iota(jnp.int32, (tm, tm), 1)
    same = (r // GDN_CHUNK) == (c // GDN_CHUNK)
    ones_bd = jnp.where(same, 1.0, 0.0).astype(BF16)
    tri_f = jnp.where(same & (c <= r), 1.0, 0.0).astype(BF16)
    tri_b = jnp.where(same & (c >= r), 1.0, 0.0).astype(BF16)
    gc = jnp.where(lane < 12, _sel_dot(tri_f, g), _sel_dot(tri_b, g))
    comp = jnp.where(lane < 8, beta, gc)
    exp_all = _dot_sel(comp, ex_ref[...])
    tot = _sel_dot(ones_bd, g)
    tot_e = _dot_sel(tot, ex_ref[:, 2 * GDN_WIDTH:4 * GDN_WIDTH])

    row = lax.broadcasted_iota(jnp.int32, (tm, GDN_WIDTH), 0)
    col = lax.broadcasted_iota(jnp.int32, (tm, GDN_WIDTH), 1)
    diag = (row % GDN_CHUNK) == (col % GDN_DK)
    for d in range(N_DIR):
        be_ref[d] = exp_all[:, d * GDN_WIDTH:(d + 1) * GDN_WIDTH]
        gce = exp_all[:, (2 + d) * GDN_WIDTH:(3 + d) * GDN_WIDTH]
        gce_ref[d] = gce
        gcr_ref[d] = _sel_dot(ones_bd, jnp.where(diag, gce, 0.0))
        kg = k * jnp.exp(tot_e[:, d * GDN_WIDTH:(d + 1) * GDN_WIDTH] - gce)
        kgt_ref[d] = kg.T.astype(BF16)


def _gdn_prep(gqkv, gates, conv_w, neg_a, dt_b, bd256, expand, seq, tm):
    n_tok = gqkv.shape[0]
    width = 3 * GDN_WIDTH
    const = lambda i: (0, 0)
    row = lambda i: (i, 0)
    drow = lambda i: (0, i, 0)
    tok = jax.ShapeDtypeStruct((n_tok, GDN_WIDTH), F32)
    dtok = jax.ShapeDtypeStruct((N_DIR, n_tok, GDN_WIDTH), F32)
    return pl.pallas_call(
        functools.partial(_gdn_prep_kernel, tps=seq // tm),
        grid=(n_tok // tm,),
        in_specs=_halo_specs(tm, width, n_tok) + [
            pl.BlockSpec((tm, LANES), row),
            pl.BlockSpec((GDN_CONV, width), const),
            pl.BlockSpec((1, LANES), const),
            pl.BlockSpec((1, LANES), const),
            pl.BlockSpec((GDN_WIDTH, GDN_WIDTH), const),
            pl.BlockSpec((LANES, 4 * GDN_WIDTH), const),
        ],
        out_specs=[
            pl.BlockSpec((tm, GDN_WIDTH), row),
            pl.BlockSpec((tm, GDN_WIDTH), row),
            pl.BlockSpec((tm, GDN_WIDTH), row),
            pl.BlockSpec((N_DIR, tm, GDN_WIDTH), drow),
            pl.BlockSpec((N_DIR, tm, GDN_WIDTH), drow),
            pl.BlockSpec((N_DIR, tm, GDN_WIDTH), drow),
            pl.BlockSpec((N_DIR, GDN_WIDTH, tm), lambda i: (0, 0, i)),
        ],
        out_shape=[tok, tok, tok, dtok, dtok, dtok,
                   jax.ShapeDtypeStruct((N_DIR, GDN_WIDTH, n_tok), BF16)],
        compiler_params=_params(("parallel",)),
        name="gdn_prep",
    )(gqkv, gqkv, gqkv, gates, conv_w, neg_a, dt_b, bd256, expand)


def _gdn_scan_kernel(qf_ref, kf_ref, vf_ref, bef_ref, gcef_ref, gcrf_ref,
                     qb_ref, kb_ref, vb_ref, beb_ref, gceb_ref, gcrb_ref, kgtf_ref, kgtb_ref,
                     of_ref, ob_ref, state_ref, stage_ref, gl_ref, *, chunks):
    cs = GDN_CHUNK
    w = GDN_WIDTH
    par = pl.program_id(1) % 2
    prv = 1 - par

    @pl.when(pl.program_id(1) == 0)
    def _():
        state_ref[...] = jnp.zeros_like(state_ref)
        stage_ref[...] = jnp.zeros_like(stage_ref)
        gl_ref[...] = jnp.zeros_like(gl_ref)

    lane_h = lax.broadcasted_iota(jnp.int32, (cs, w), 1) // GDN_DK
    heads = [lane_h == h for h in range(GDN_HEADS)]
    ti = lax.broadcasted_iota(jnp.int32, (cs, w), 0)
    tj = lax.broadcasted_iota(jnp.int32, (cs, w), 1) % cs
    eye = (ti == tj).astype(F32)
    sr = lax.broadcasted_iota(jnp.int32, (w, w), 0) // GDN_DK
    sc = lax.broadcasted_iota(jnp.int32, (w, w), 1) // GDN_DK
    state_mask = sr == sc
    zeros = jnp.zeros((cs, w), F32)
    in_refs = ((qf_ref, kf_ref, vf_ref, bef_ref, gcef_ref, gcrf_ref),
               (qb_ref, kb_ref, vb_ref, beb_ref, gceb_ref, gcrb_ref))
    kgt_refs = (kgtf_ref, kgtb_ref)
    o_refs = (of_ref, ob_ref)
    U, W, QG, QKM = range(4)

    def blockdiag(y):
        return jnp.concatenate([jnp.where(hm, y, 0.0) for hm in heads], axis=0).astype(BF16)

    def rows_of(ci):
        return slice(ci * cs, (ci + 1) * cs)

    loc = {}

    def local_scores(d, ci):
        q_ref, k_ref, v_ref, be_ref, gce_ref, gcr_ref = in_refs[d]
        rs = rows_of(ci)
        q = q_ref[rs, :]
        k = k_ref[rs, :]
        be = be_ref[0, rs, :]
        kb = k * be
        kq = _dot_nt(jnp.concatenate([kb, q], axis=0).astype(BF16), blockdiag(k))
        loc[d, ci] = dict(kq=kq, kb=kb)

    def local_decay(d, ci):
        st = loc[d, ci]
        _, _, _, _, gce_ref, gcr_ref = in_refs[d]
        rs = rows_of(ci)
        incl = (ti >= tj) if d == 0 else (ti <= tj)
        strict = (ti > tj) if d == 0 else (ti < tj)
        diff = gce_ref[0, rs, :] - gcr_ref[0, rs, :]
        decay = jnp.where(incl, jnp.exp(jnp.where(incl, diff, 0.0)), 0.0)
        kq = st.pop("kq")
        a = jnp.where(strict, kq[0:cs] * decay, 0.0)
        stage_ref[par, d, QKM, rs, :] = kq[cs:2 * cs] * decay
        st["pw"] = _dot(a.astype(BF16), blockdiag(a))
        st["t"] = eye - a

    def local_invert(d, ci, last):
        st = loc[d, ci]
        bdp = blockdiag(st["pw"])
        if last:
            st["t"] = st["t"] + _dot(st["t"].astype(BF16), bdp)
        else:
            both = _dot(jnp.concatenate([st["pw"], st["t"]], axis=0).astype(BF16), bdp)
            st["pw"] = both[0:cs]
            st["t"] = st["t"] + both[cs:2 * cs]

    def local_uw(d, ci):
        st = loc.pop((d, ci))
        q_ref, _, v_ref, be_ref, gce_ref, _ = in_refs[d]
        rs = rows_of(ci)
        eg = jnp.exp(gce_ref[0, rs, :])
        vb = v_ref[rs, :] * be_ref[0, rs, :]
        rhs = jnp.concatenate([blockdiag(vb), blockdiag(st["kb"] * eg)], axis=1)
        uw = _dot(st["t"].astype(BF16), rhs)
        stage_ref[par, d, U, rs, :] = uw[:, 0:w]
        stage_ref[par, d, W, rs, :] = uw[:, w:2 * w]
        stage_ref[par, d, QG, rs, :] = q_ref[rs, :] * eg
        edge = (ci + 1) * cs - 1 if d == 0 else ci * cs
        gl_ref[par, d, ci] = jnp.exp(gce_ref[0, edge:edge + 1, :])

    scan = {}

    def scan_read(d, ci):
        rs = rows_of(ci)
        lhs = jnp.concatenate([stage_ref[prv, d, W, rs, :], stage_ref[prv, d, QG, rs, :]], axis=0)
        scan[d] = _dot(lhs.astype(BF16), state_ref[d].astype(BF16))

    def scan_out(d, ci):
        rs = rows_of(ci)
        ws = scan[d]
        vn = stage_ref[prv, d, U, rs, :] - ws[0:cs]
        o_refs[d][rs, :] = ws[cs:2 * cs] + _dot(stage_ref[prv, d, QKM, rs, :].astype(BF16), blockdiag(vn))
        scan[d] = vn

    def scan_update(d, ci):
        vn = scan[d]
        pair = jnp.concatenate([vn, zeros] if ci % 2 == 0 else [zeros, vn], axis=0).astype(BF16)
        kgt = kgt_refs[d][0, :, (ci // 2) * LANES:(ci // 2 + 1) * LANES]
        upd = jnp.where(state_mask, _dot(kgt, pair), 0.0)
        state_ref[d] = state_ref[d] * gl_ref[prv, d, ci] + upd

    order = [(d, ci if d == 0 else chunks - 1 - ci) for ci in range(chunks) for d in range(N_DIR)]
    local_ops = [functools.partial(local_scores, d, ci) for d, ci in order]
    local_ops += [functools.partial(local_decay, d, ci) for d, ci in order]
    for step in range(5):
        local_ops += [functools.partial(local_invert, d, ci, step == 4) for d, ci in order]
    local_ops += [functools.partial(local_uw, d, ci) for d, ci in order]
    scan_ops = []
    for ci in range(chunks):
        for fn in (scan_read, scan_out, scan_update):
            scan_ops.append([functools.partial(fn, d, ci if d == 0 else chunks - 1 - ci) for d in range(N_DIR)])
    every = len(local_ops) // len(scan_ops)
    for idx, op in enumerate(local_ops):
        op()
        if (idx + 1) % every == 0 and scan_ops:
            for s_op in scan_ops.pop(0):
                s_op()
    for level in scan_ops:
        for s_op in level:
            s_op()


def _gdn_scan(q, k, v, be, gce, gcr, kgt, batch, seq, chunks):
    n_tok = q.shape[0]
    rows = chunks * GDN_CHUNK
    ng = seq // rows
    w = GDN_WIDTH

    def group(d, lag):
        def f(b, n):
            g = jnp.clip(n - lag, 0, ng - 1)
            return b * ng + (ng - 1 - g if d == 1 else g)
        return f

    def tok(d, lag):
        f = group(d, lag)
        return lambda b, n: (f(b, n), 0)

    def dtok(d, lag):
        f = group(d, lag)
        return lambda b, n: (d, f(b, n), 0)

    def dlane(d, lag):
        f = group(d, lag)
        return lambda b, n: (d, 0, f(b, n))

    def local_specs(d):
        return [pl.BlockSpec((rows, w), tok(d, 0))] * 3 + [pl.BlockSpec((1, rows, w), dtok(d, 0))] * 3

    out = jax.ShapeDtypeStruct((n_tok, w), F32)
    return pl.pallas_call(
        functools.partial(_gdn_scan_kernel, chunks=chunks),
        grid=(batch, ng + 1),
        in_specs=local_specs(0) + local_specs(1) + [pl.BlockSpec((1, w, rows), dlane(d, 1)) for d in range(N_DIR)],
        out_specs=[pl.BlockSpec((rows, w), tok(d, 1)) for d in range(N_DIR)],
        out_shape=[out, out],
        scratch_shapes=[pltpu.VMEM((N_DIR, w, w), F32),
                        pltpu.VMEM((2, N_DIR, 4, rows, w), F32),
                        pltpu.VMEM((2, N_DIR, chunks, 1, w), F32)],
        compiler_params=_params(("parallel", "arbitrary")),
        name="gdn_scan",
    )(q, k, v, be, gce, gcr, q, k, v, be, gce, gcr, kgt, kgt)


def _outproj_kernel(yp_ref, ya_ref, of_ref, ob_ref, z_ref, h_ref, gw_ref, bd_ref,
                    w_ref, nw_ref, o_ref):
    a0 = POOL_WIDTH
    a1 = POOL_WIDTH + ATTN_WIDTH
    mix = _dot(yp_ref[...], w_ref[0:a0, :]) + _dot(ya_ref[...], w_ref[a0:a1, :])
    o = of_ref[...] + ob_ref[...]
    ms = _group_sum(o * o, bd_ref[...]) * (1.0 / GDN_DK)
    z = z_ref[...]
    yg = o * lax.rsqrt(ms + EPS) * gw_ref[...] * (z / (1.0 + jnp.exp(-z)))
    mix = mix + _dot(yg.astype(BF16), w_ref[a1:D_MODEL, :])
    o_ref[...] = h_ref[...] + _rms(mix, nw_ref[...])


def _outproj(y_pool, y_attn, o_f, o_b, z, h, gdn_nw, bd256, w_out, post_w, tm):
    n_tok = h.shape[0]
    const = lambda i: (0, 0)
    row = lambda i: (i, 0)
    return pl.pallas_call(
        _outproj_kernel,
        grid=(n_tok // tm,),
        in_specs=[
            pl.BlockSpec((tm, POOL_WIDTH), row),
            pl.BlockSpec((tm, ATTN_WIDTH), row),
            pl.BlockSpec((tm, GDN_WIDTH), row),
            pl.BlockSpec((tm, GDN_WIDTH), row),
            pl.BlockSpec((tm, GDN_WIDTH), row),
            pl.BlockSpec((tm, D_MODEL), row),
            pl.BlockSpec((1, GDN_WIDTH), const),
            pl.BlockSpec((GDN_WIDTH, GDN_WIDTH), const),
            pl.BlockSpec((D_MODEL, D_MODEL), const),
            pl.BlockSpec((1, D_MODEL), const),
        ],
        out_specs=pl.BlockSpec((tm, D_MODEL), row),
        out_shape=jax.ShapeDtypeStruct((n_tok, D_MODEL), F32),
        compiler_params=_params(("parallel",)),
        name="outproj",
    )(y_pool, y_attn, o_f, o_b, z, h, gdn_nw, bd256, w_out, post_w)


def _ffn_kernel(prev_ref, h_ref, next_ref, prew_ref, wg_ref, wv_ref, cg_ref, cv_ref, wd_ref, postw_ref,
                o_ref, act_ref, *, tps, chunk):
    tm = h_ref.shape[0]
    _, first, last = _seq_edges(tps)
    h = h_ref[...]
    prew = prew_ref[...]
    prev = jnp.where(first, 0.0, _rms(prev_ref[...], prew))
    nxt = jnp.where(last, 0.0, _rms(next_ref[...], prew))
    xn = jnp.concatenate([prev, _rms(h, prew), nxt], axis=0).astype(BF16)
    sl = slice(HALO, HALO + tm)

    def conv3(up, cw):
        out = _shift_rows(up, -1) * cw[0:1] + up * cw[1:2] + _shift_rows(up, 1) * cw[2:3]
        return out[sl]

    for c in range(D_FF // chunk):
        cs = slice(c * chunk, (c + 1) * chunk)
        gate = conv3(_dot(xn, wg_ref[:, cs]), cg_ref[:, cs])
        val = conv3(_dot(xn, wv_ref[:, cs]), cv_ref[:, cs])
        inner = 0.7978845608028654 * (gate + 0.044715 * (gate * gate * gate))
        act_ref[:, cs] = (0.5 * gate * (1.0 + jnp.tanh(inner)) * val).astype(BF16)
    ff = _dot(act_ref[...], wd_ref[...])
    o_ref[...] = h + _rms(ff, postw_ref[...])


def _ffn(h, pre_w, w_up, conv_w, w_down, post_w, seq, tm, chunk):
    n_tok = h.shape[0]
    const = lambda i: (0, 0)
    second = lambda i: (0, 1)
    return pl.pallas_call(
        functools.partial(_ffn_kernel, tps=seq // tm, chunk=chunk),
        grid=(n_tok // tm,),
        in_specs=_halo_specs(tm, D_MODEL, n_tok) + [
            pl.BlockSpec((1, D_MODEL), const),
            pl.BlockSpec((D_MODEL, D_FF), const),
            pl.BlockSpec((D_MODEL, D_FF), second),
            pl.BlockSpec((FFN_CONV, D_FF), const),
            pl.BlockSpec((FFN_CONV, D_FF), second),
            pl.BlockSpec((D_FF, D_MODEL), const),
            pl.BlockSpec((1, D_MODEL), const),
        ],
        out_specs=pl.BlockSpec((tm, D_MODEL), lambda i: (i, 0)),
        out_shape=jax.ShapeDtypeStruct((n_tok, D_MODEL), F32),
        scratch_shapes=[pltpu.VMEM((tm, D_FF), BF16)],
        compiler_params=_params(("parallel",)),
        name="ffn",
    )(h, h, h, pre_w, w_up, w_up, conv_w, conv_w, w_down, post_w)


def _rope_tables(seq):
    t = np.arange(seq)
    pos = np.stack([t // GRID_W, t % GRID_W], axis=-1).astype(np.float32)
    axis_dim = HEAD_DIM // 2
    inv_freq = (ROPE_THETA ** (-np.arange(0, axis_dim, 2, dtype=np.float32) / axis_dim)).astype(np.float32)
    ang = pos[:, :, None] * inv_freq
    j = np.arange(LANES) % HEAD_DIM
    a = ang[:, j // 32, j % 16]
    sign = np.where((j % 32) < 16, -1.0, 1.0).astype(np.float32)
    return jnp.asarray(np.cos(a), F32), jnp.asarray(np.sin(a) * sign, F32)


def _head_blockdiag(n):
    i = np.arange(n) // HEAD_DIM
    return jnp.asarray(i[:, None] == i[None, :], BF16)


def _gate_expand():
    e = np.zeros((LANES, 4 * GDN_WIDTH), np.float32)
    for kind in range(2):
        for d in range(N_DIR):
            for hh in range(GDN_HEADS):
                src = kind * N_DIR * GDN_HEADS + d * GDN_HEADS + hh
                dst = (kind * N_DIR + d) * GDN_WIDTH + hh * GDN_DK
                e[src, dst:dst + GDN_DK] = 1.0
    return jnp.asarray(e, BF16)


def _pad_lanes(v, offset):
    out = jnp.zeros((1, LANES), F32)
    return out.at[0, offset:offset + v.shape[0]].set(v.astype(F32))


def kernel(x, pre_mix_norm, post_mix_norm, pre_ffn_norm, post_ffn_norm, w_in, pool_w, pool_scale, q_norm, k_norm, gdn_conv, gdn_a_log, gdn_dt_bias, gdn_norm, w_out, w_up, ffn_conv, w_down):
    batch, seq, _ = x.shape
    n_tok = batch * seq
    depth = w_in.shape[0]
    assert seq % 512 == 0 and seq % GRID_W == 0

    tm_proj = 512
    tm_pool = 512
    tm_gdn = 256
    tm_ffn = 512
    cos_t, sin_t = _rope_tables(seq)
    bd256 = _head_blockdiag(GDN_WIDTH)
    expand = _gate_expand()
    n_gate = N_DIR * GDN_HEADS

    h = x.reshape(n_tok, D_MODEL)
    for l in range(depth):
        w_cat = jnp.pad(w_in[l], ((0, 0), (0, N_CAT - D_IN))).astype(BF16)
        qw = jnp.tile(q_norm[l], 2)[None, :]
        kw = jnp.tile(k_norm[l], 2)[None, :]
        u_pool, qt, k4, vt, gqkv, z, gates = _inproj(
            h, pre_mix_norm[l][None, :], w_cat, cos_t, sin_t, qw, kw, bd256, seq, tm_proj)

        bdw = jax.scipy.linalg.block_diag(*[pool_w[l, g] for g in range(POOL_GROUPS)]).astype(BF16)
        y_pool = _pool(u_pool, bdw, pool_scale[l][None, :], seq, tm_pool)

        y_attn = _attention(qt, k4, vt, batch, seq, 256, 256, 16)

        neg_a = _pad_lanes(-jnp.exp(gdn_a_log[l].reshape(-1)), n_gate)
        dt_b = _pad_lanes(gdn_dt_bias[l].reshape(-1), n_gate)
        gq, gk, gv, be, gce, gcr, kgt = _gdn_prep(gqkv, gates, gdn_conv[l], neg_a, dt_b, bd256, expand, seq, tm_gdn)
        o_f, o_b = _gdn_scan(gq, gk, gv, be, gce, gcr, kgt, batch, seq, 4)

        h = _outproj(y_pool, y_attn, o_f, o_b, z, h, jnp.tile(gdn_norm[l], GDN_HEADS)[None, :], bd256,
                     w_out[l].astype(BF16), post_mix_norm[l][None, :], 512)

        h = _ffn(h, pre_ffn_norm[l][None, :], w_up[l].astype(BF16), ffn_conv[l],
                 w_down[l].astype(BF16), post_ffn_norm[l][None, :], seq, tm_ffn, 256)
    return h.reshape(batch, seq, D_MODEL)
```

```python
import functools
import math

import numpy as np
import jax
import jax.numpy as jnp
from jax import lax
from jax.experimental import pallas as pl
from jax.experimental.pallas import tpu as pltpu

F32 = jnp.float32
BF16 = jnp.bfloat16

D_MODEL = 1024
GRID_W = 64
EPS = 1e-6
POOL_WIDTH = 256
POOL_WINDOWS = (2, 4, 8, 16)
POOL_GROUPS = 4
POOL_GDIM = POOL_WIDTH // POOL_GROUPS
ATTN_HEADS = 8
ATTN_KV_HEADS = 2
HEAD_DIM = 64
ATTN_WIDTH = ATTN_HEADS * HEAD_DIM
KV_WIDTH = ATTN_KV_HEADS * HEAD_DIM
ROPE_THETA = 10000.0
GDN_HEADS = 4
GDN_DK = 64
GDN_WIDTH = GDN_HEADS * GDN_DK
GDN_CONV = 5
GDN_CHUNK = 64
N_DIR = 2
D_FF = 2816
FFN_CONV = 3
D_IN = POOL_WIDTH + ATTN_WIDTH + 2 * KV_WIDTH + 4 * GDN_WIDTH + 2 * N_DIR * GDN_HEADS

LANES = 128
SUBLANES = 8
HALO = SUBLANES
VMEM_LIMIT = 56 * 1024 * 1024

N_CAT = 2176
COL_POOL = 0
COL_QA = COL_POOL + POOL_WIDTH
COL_KA = COL_QA + ATTN_WIDTH
COL_VA = COL_KA + KV_WIDTH
COL_GQKV = COL_VA + KV_WIDTH
COL_Z = COL_GQKV + 3 * GDN_WIDTH
COL_GATES = COL_Z + GDN_WIDTH

Q_SCALE = HEAD_DIM ** -0.5 * math.log2(math.e)
V_ROWS = HEAD_DIM + 16


def _dot(a, b):
    return jnp.dot(a, b, preferred_element_type=F32)


def _dot_nt(a, b):
    return lax.dot_general(a, b, (((1,), (1,)), ((), ())), preferred_element_type=F32)


def _split2(x):
    hi = x.astype(BF16)
    lo = (x - hi.astype(F32)).astype(BF16)
    return hi, lo


def _sel_dot(m01, x):
    hi, lo = _split2(x)
    return _dot(m01, hi) + _dot(m01, lo)


def _dot_sel(x, m01):
    hi, lo = _split2(x)
    return _dot(hi, m01) + _dot(lo, m01)


def _group_sum(x, bd):
    return _dot_sel(x, bd)


def _rms(x, w):
    ms = jnp.mean(x * x, axis=-1, keepdims=True)
    return x * lax.rsqrt(ms + EPS) * w


def _shift_rows(x, d):
    n = x.shape[0]
    return pltpu.roll(x, (-d) % n, 0)


def _params(sem):
    return pltpu.CompilerParams(dimension_semantics=sem, vmem_limit_bytes=VMEM_LIMIT)


def _halo_specs(tm, width, n_rows):
    per = tm // HALO
    last = n_rows // HALO - 1
    return [
        pl.BlockSpec((HALO, width), lambda i: (jnp.maximum(i * per - 1, 0), 0)),
        pl.BlockSpec((tm, width), lambda i: (i, 0)),
        pl.BlockSpec((HALO, width), lambda i: (jnp.minimum((i + 1) * per, last), 0)),
    ]


def _seq_edges(tiles_per_seq):
    t = pl.program_id(0) % tiles_per_seq
    return t, t == 0, t == tiles_per_seq - 1


def _inproj_kernel(x_ref, nw_ref, w_ref, cos_ref, sin_ref, qw_ref, kw_ref, bd_ref,
                   up_ref, q_ref, k_ref, vt_ref, g_ref, z_ref, gt_ref):
    tm = x_ref.shape[0]
    xn = _rms(x_ref[...], nw_ref[...]).astype(BF16)

    cos = cos_ref[...]
    sin = sin_ref[...]
    lane = lax.broadcasted_iota(jnp.int32, (tm, LANES), 1)
    first_half = (lane & 16) == 0
    lo64 = lane < HEAD_DIM
    bd = bd_ref[...]

    def head_norm_rope(x, ss, w):
        y = x * lax.rsqrt(ss * (1.0 / HEAD_DIM) + EPS) * w
        partner = jnp.where(first_half, pltpu.roll(y, LANES - 16, 1), pltpu.roll(y, 16, 1))
        return y * cos + partner * sin

    n_slab = ATTN_WIDTH // LANES
    qa = _dot(xn, w_ref[:, COL_QA:COL_KA])
    ka = _dot(xn, w_ref[:, COL_KA:COL_VA])
    ss_q = [_group_sum(qa[:, p * 2 * LANES:(p + 1) * 2 * LANES] ** 2, bd) for p in range(n_slab // 2)]
    ss_q = [ss_q[s // 2][:, (s % 2) * LANES:(s % 2 + 1) * LANES] for s in range(n_slab)]
    qa = [qa[:, s * LANES:(s + 1) * LANES] for s in range(n_slab)]
    ss_k = _group_sum(ka * ka, bd[0:LANES, 0:LANES])
    up_ref[...] = _dot(xn, w_ref[:, COL_POOL:COL_QA])

    qw = qw_ref[...] * Q_SCALE
    q_rot = jnp.concatenate([head_norm_rope(qa[s], ss_q[s], qw) for s in range(n_slab)], axis=1)
    q_ref[...] = q_rot.T.astype(BF16)

    kr = head_norm_rope(ka, ss_k, kw_ref[...])
    ks = pltpu.roll(kr, HEAD_DIM, 1)
    k_ref[:, 0:128] = jnp.where(lo64, kr, 0.0).astype(BF16)
    k_ref[:, 128:256] = jnp.where(lo64, 0.0, ks).astype(BF16)
    k_ref[:, 256:384] = jnp.where(lo64, ks, 0.0).astype(BF16)
    k_ref[:, 384:512] = jnp.where(lo64, 0.0, kr).astype(BF16)

    vat = _dot(xn, w_ref[:, COL_VA:COL_GQKV]).T
    pad_row = lax.broadcasted_iota(jnp.int32, (V_ROWS - HEAD_DIM, tm), 0)
    ones_rows = jnp.where(pad_row == 0, 1.0, 0.0)
    vt_ref[...] = jnp.concatenate(
        [vat[0:HEAD_DIM], ones_rows, vat[HEAD_DIM:2 * HEAD_DIM], ones_rows], axis=0).astype(BF16)
    g_ref[...] = _dot(xn, w_ref[:, COL_GQKV:COL_Z])
    z_ref[...] = _dot(xn, w_ref[:, COL_Z:COL_GATES])
    gt_ref[...] = _dot(xn, w_ref[:, COL_GATES:N_CAT])


def _inproj(x2d, norm_w, w_cat, cos_t, sin_t, qw, kw, bd256, seq, tm):
    n_tok = x2d.shape[0]
    tps = seq // tm
    const = lambda i: (0, 0)
    row = lambda i: (i, 0)
    pos = lambda i: (i % tps, 0)
    return pl.pallas_call(
        _inproj_kernel,
        grid=(n_tok // tm,),
        in_specs=[
            pl.BlockSpec((tm, D_MODEL), row),
            pl.BlockSpec((1, D_MODEL), const),
            pl.BlockSpec((D_MODEL, N_CAT), const),
            pl.BlockSpec((tm, LANES), pos),
            pl.BlockSpec((tm, LANES), pos),
            pl.BlockSpec((1, LANES), const),
            pl.BlockSpec((1, LANES), const),
            pl.BlockSpec((2 * LANES, 2 * LANES), const),
        ],
        out_specs=[
            pl.BlockSpec((tm, POOL_WIDTH), row),
            pl.BlockSpec((ATTN_WIDTH, tm), lambda i: (0, i)),
            pl.BlockSpec((tm, 4 * LANES), row),
            pl.BlockSpec((ATTN_KV_HEADS * V_ROWS, tm), lambda i: (0, i)),
            pl.BlockSpec((tm, 3 * GDN_WIDTH), row),
            pl.BlockSpec((tm, GDN_WIDTH), row),
            pl.BlockSpec((tm, LANES), row),
        ],
        out_shape=[
            jax.ShapeDtypeStruct((n_tok, POOL_WIDTH), F32),
            jax.ShapeDtypeStruct((ATTN_WIDTH, n_tok), BF16),
            jax.ShapeDtypeStruct((n_tok, 4 * LANES), BF16),
            jax.ShapeDtypeStruct((ATTN_KV_HEADS * V_ROWS, n_tok), BF16),
            jax.ShapeDtypeStruct((n_tok, 3 * GDN_WIDTH), F32),
            jax.ShapeDtypeStruct((n_tok, GDN_WIDTH), F32),
            jax.ShapeDtypeStruct((n_tok, LANES), F32),
        ],
        compiler_params=_params(("parallel",)),
        name="inproj",
    )(x2d, norm_w, w_cat, cos_t, sin_t, qw, kw, bd256)


def _pool_kernel(prev_ref, u_ref, next_ref, bdw_ref, sc_ref, o_ref, *, tps, seq):
    tm = u_ref.shape[0]
    t, first, last = _seq_edges(tps)
    u = u_ref[...]
    prev = jnp.where(first, 0.0, prev_ref[...])
    nxt = jnp.where(last, 0.0, next_ref[...])
    ext = jnp.concatenate([prev, u, nxt], axis=0)
    w2 = ext + _shift_rows(ext, -1)
    w4 = _shift_rows(w2, -1) + _shift_rows(w2, 1)
    w8 = _shift_rows(w4, -2) + _shift_rows(w4, 2)
    w16 = _shift_rows(w8, -4) + _shift_rows(w8, 4)
    sl = slice(HALO, HALO + tm)
    grp = lax.broadcasted_iota(jnp.int32, (tm, POOL_WIDTH), 1) // POOL_GDIM
    seg = jnp.where(grp == 0, w2[sl], jnp.where(grp == 1, w4[sl], jnp.where(grp == 2, w8[sl], w16[sl])))
    half = jnp.where(grp == 0, 1, jnp.where(grp == 1, 2, jnp.where(grp == 2, 4, 8)))
    pos = t * tm + lax.broadcasted_iota(jnp.int32, (tm, POOL_WIDTH), 0)
    cnt = (jnp.minimum(pos + half, seq) - jnp.maximum(pos - half, 0)).astype(F32)
    pooled = seg / cnt - u
    o_ref[...] = (_dot(pooled.astype(BF16), bdw_ref[...]) * sc_ref[...]).astype(BF16)


def _pool(u, bdw, scale, seq, tm):
    n_tok = u.shape[0]
    const = lambda i: (0, 0)
    return pl.pallas_call(
        functools.partial(_pool_kernel, tps=seq // tm, seq=seq),
        grid=(n_tok // tm,),
        in_specs=_halo_specs(tm, POOL_WIDTH, n_tok) + [
            pl.BlockSpec((POOL_WIDTH, POOL_WIDTH), const),
            pl.BlockSpec((1, POOL_WIDTH), const),
        ],
        out_specs=pl.BlockSpec((tm, POOL_WIDTH), lambda i: (i, 0)),
        out_shape=jax.ShapeDtypeStruct((n_tok, POOL_WIDTH), BF16),
        compiler_params=_params(("parallel",)),
        name="pool",
    )(u, u, u, bdw, scale)


def _attn_kernel(q_ref, k_ref, vt_ref, o_ref, acc_ref, m_ref, s_ref, bm_ref, *, tk, per_iter):
    seq = k_ref.shape[0]
    group = ATTN_HEADS // ATTN_KV_HEADS
    n_chunks = seq // tk
    acc_ref[...] = jnp.zeros_like(acc_ref)
    m_ref[...] = jnp.full_like(m_ref, -1e30)

    def scores(c, buf, slab):
        r0 = pl.multiple_of(c * tk, tk)
        kk = jnp.concatenate([k_ref[pl.ds(r0, tk), 0:LANES], k_ref[pl.ds(r0, tk), LANES:2 * LANES]], axis=0)
        s2 = _dot(kk, q_ref[slab * LANES:(slab + 1) * LANES, :])
        for half in range(2):
            j = 2 * slab + half
            s = s2[half * tk:(half + 1) * tk]
            s_ref[buf, j] = s
            bm_ref[buf, j] = jnp.max(s, axis=0, keepdims=True)

    def softmax_pv(c, buf, slab):
        r0 = pl.multiple_of(c * tk, tk)
        vt = vt_ref[:, pl.ds(r0, tk)]
        for j in (2 * slab, 2 * slab + 1):
            m = m_ref[j]
            m_new = jnp.maximum(m, bm_ref[buf, j])
            alpha = jnp.exp2(m - m_new)
            p = jnp.exp2(s_ref[buf, j] - m_new)
            m_ref[j] = m_new
            acc_ref[j] = alpha * acc_ref[j] + _dot(vt, p.astype(BF16))

    slabs = range(group // 2)
    for slab in slabs:
        scores(0, 0, slab)

    def body(i, carry):
        c0 = per_iter * i
        for u in range(per_iter):
            for slab in slabs:
                scores(jnp.minimum(c0 + u + 1, n_chunks - 1), (u + 1) % 2, slab)
            for slab in slabs:
                softmax_pv(c0 + u, u % 2, slab)
        return carry

    lax.fori_loop(0, n_chunks // per_iter, body, 0)
    out = jnp.concatenate([acc_ref[j, 0:HEAD_DIM] / acc_ref[j, HEAD_DIM:HEAD_DIM + 1] for j in range(group)], axis=0)
    o_ref[...] = out.T.astype(BF16)


def _attention(qt, k4, vt, batch, seq, tq, tk, per_iter):
    n_tok = k4.shape[0]
    nq = seq // tq
    group = ATTN_HEADS // ATTN_KV_HEADS
    gw = group * HEAD_DIM
    per_iter = min(per_iter, seq // tk)
    assert per_iter % 2 == 0 and (seq // tk) % per_iter == 0
    return pl.pallas_call(
        functools.partial(_attn_kernel, tk=tk, per_iter=per_iter),
        grid=(batch, ATTN_KV_HEADS, nq),
        in_specs=[
            pl.BlockSpec((gw, tq), lambda b, g, i: (g, b * nq + i)),
            pl.BlockSpec((seq, 2 * LANES), lambda b, g, i: (b, g)),
            pl.BlockSpec((V_ROWS, seq), lambda b, g, i: (g, b)),
        ],
        out_specs=pl.BlockSpec((tq, gw), lambda b, g, i: (b * nq + i, g)),
        out_shape=jax.ShapeDtypeStruct((n_tok, ATTN_WIDTH), BF16),
        scratch_shapes=[pltpu.VMEM((group, V_ROWS, tq), F32),
                        pltpu.VMEM((group, 1, tq), F32),
                        pltpu.VMEM((2, group, tk, tq), F32),
                        pltpu.VMEM((2, group, 1, tq), F32)],
        compiler_params=_params(("parallel", "parallel", "parallel")),
        name="attention",
    )(qt, k4, vt)


def _gdn_prep_kernel(prev_ref, x_ref, next_ref, gt_ref, cw_ref, na_ref, dtb_ref, bd_ref, ex_ref,
                     q_ref, k_ref, v_ref, be_ref, gce_ref, gcr_ref, kgt_ref, *, tps):
    tm = x_ref.shape[0]
    _, first, last = _seq_edges(tps)
    prev = jnp.where(first, 0.0, prev_ref[...])
    nxt = jnp.where(last, 0.0, next_ref[...])
    ext = jnp.concatenate([prev, x_ref[...], nxt], axis=0)
    cw = cw_ref[...]
    conv = ext * cw[2:3]
    for kk in (0, 1, 3, 4):
        conv = conv + _shift_rows(ext, kk - GDN_CONV // 2) * cw[kk:kk + 1]
    conv = conv[HALO:HALO + tm]
    act = conv / (1.0 + jnp.exp(-conv))
    bd = bd_ref[...]

    def l2n(x):
        return x * lax.rsqrt(_group_sum(x * x, bd) + EPS)

    q = l2n(act[:, 0:GDN_WIDTH]) * (GDN_DK ** -0.5)
    k = l2n(act[:, GDN_WIDTH:2 * GDN_WIDTH])
    q_ref[...] = q
    k_ref[...] = k
    v_ref[...] = act[:, 2 * GDN_WIDTH:3 * GDN_WIDTH]

    gt = gt_ref[...]
    lane = lax.broadcasted_iota(jnp.int32, (tm, LANES), 1)
    beta = 1.0 / (1.0 + jnp.exp(-gt))
    xs = gt + dtb_ref[...]
    softplus = jnp.maximum(xs, 0.0) + jnp.log1p(jnp.exp(-jnp.abs(xs)))
    g = na_ref[...] * softplus

    r = lax.broadcasted_iota(jnp.int32, (tm, tm), 0)
    c = lax.broadcasted_iota(jnp.int32, (tm, tm), 1)
    same = (r // GDN_CHUNK) == (c // GDN_CHUNK)
    ones_bd = jnp.where(same, 1.0, 0.0).astype(BF16)
    tri_f = jnp.where(same & (c <= r), 1.0, 0.0).astype(BF16)
    tri_b = jnp.where(same & (c >= r), 1.0, 0.0).astype(BF16)
    gc = jnp.where(lane < 12, _sel_dot(tri_f, g), _sel_dot(tri_b, g))
    comp = jnp.where(lane < 8, beta, gc)
    exp_all = _dot_sel(comp, ex_ref[...])
    tot = _sel_dot(ones_bd, g)
    tot_e = _dot_sel(tot, ex_ref[:, 2 * GDN_WIDTH:4 * GDN_WIDTH])

    row = lax.broadcasted_iota(jnp.int32, (tm, GDN_WIDTH), 0)
    col = lax.broadcasted_iota(jnp.int32, (tm, GDN_WIDTH), 1)
    diag = (row % GDN_CHUNK) == (col % GDN_DK)
    for d in range(N_DIR):
        be_ref[d] = exp_all[:, d * GDN_WIDTH:(d + 1) * GDN_WIDTH]
        gce = exp_all[:, (2 + d) * GDN_WIDTH:(3 + d) * GDN_WIDTH]
        gce_ref[d] = gce
        gcr_ref[d] = _sel_dot(ones_bd, jnp.where(diag, gce, 0.0))
        kg = k * jnp.exp(tot_e[:, d * GDN_WIDTH:(d + 1) * GDN_WIDTH] - gce)
        lo_lane = lax.broadcasted_iota(jnp.int32, (GDN_DK, LANES), 1) < GDN_DK
        for pr in range(tm // LANES):
            xt = kg[pr * LANES:(pr + 1) * LANES, :].T
            xr = pltpu.roll(xt, GDN_CHUNK, 1)
            for cl in range(2):
                even, odd = (xt, xr) if cl == 0 else (xr, xt)
                tiles = [jnp.where(lo_lane, even[(2 * p) * GDN_DK:(2 * p + 1) * GDN_DK],
                                   odd[(2 * p + 1) * GDN_DK:(2 * p + 2) * GDN_DK]) for p in range(GDN_HEADS // 2)]
                r0 = (2 * pr + cl) * GDN_CHUNK
                kgt_ref[d, r0:r0 + GDN_CHUNK, :] = jnp.concatenate(tiles, axis=1).astype(BF16)


def _gdn_prep(gqkv, gates, conv_w, neg_a, dt_b, bd256, expand, seq, tm):
    n_tok = gqkv.shape[0]
    width = 3 * GDN_WIDTH
    const = lambda i: (0, 0)
    row = lambda i: (i, 0)
    drow = lambda i: (0, i, 0)
    tok = jax.ShapeDtypeStruct((n_tok, GDN_WIDTH), F32)
    dtok = jax.ShapeDtypeStruct((N_DIR, n_tok, GDN_WIDTH), F32)
    return pl.pallas_call(
        functools.partial(_gdn_prep_kernel, tps=seq // tm),
        grid=(n_tok // tm,),
        in_specs=_halo_specs(tm, width, n_tok) + [
            pl.BlockSpec((tm, LANES), row),
            pl.BlockSpec((GDN_CONV, width), const),
            pl.BlockSpec((1, LANES), const),
            pl.BlockSpec((1, LANES), const),
            pl.BlockSpec((GDN_WIDTH, GDN_WIDTH), const),
            pl.BlockSpec((LANES, 4 * GDN_WIDTH), const),
        ],
        out_specs=[
            pl.BlockSpec((tm, GDN_WIDTH), row),
            pl.BlockSpec((tm, GDN_WIDTH), row),
            pl.BlockSpec((tm, GDN_WIDTH), row),
            pl.BlockSpec((N_DIR, tm, GDN_WIDTH), drow),
            pl.BlockSpec((N_DIR, tm, GDN_WIDTH), drow),
            pl.BlockSpec((N_DIR, tm, GDN_WIDTH), drow),
            pl.BlockSpec((N_DIR, tm, GDN_WIDTH), drow),
        ],
        out_shape=[tok, tok, tok, dtok, dtok, dtok,
                   jax.ShapeDtypeStruct((N_DIR, n_tok, GDN_WIDTH), BF16)],
        compiler_params=_params(("parallel",)),
        name="gdn_prep",
    )(gqkv, gqkv, gqkv, gates, conv_w, neg_a, dt_b, bd256, expand)


def _gdn_scan_kernel(qf_ref, kf_ref, vf_ref, bef_ref, gcef_ref, gcrf_ref,
                     qb_ref, kb_ref, vb_ref, beb_ref, gceb_ref, gcrb_ref, kgpf_ref, kgpb_ref,
                     of_ref, ob_ref, state_ref, stage_ref, gl_ref, *, chunks, wave):
    cs = GDN_CHUNK
    w = GDN_WIDTH
    par = pl.program_id(1) % 2
    prv = 1 - par

    @pl.when(pl.program_id(1) == 0)
    def _():
        state_ref[...] = jnp.zeros_like(state_ref)
        stage_ref[...] = jnp.zeros_like(stage_ref)
        gl_ref[...] = jnp.zeros_like(gl_ref)

    lane_h = lax.broadcasted_iota(jnp.int32, (cs, w), 1) // GDN_DK
    heads = [lane_h == h for h in range(GDN_HEADS)]
    ti = lax.broadcasted_iota(jnp.int32, (cs, w), 0)
    tj = lax.broadcasted_iota(jnp.int32, (cs, w), 1) % cs
    eye = (ti == tj).astype(F32)
    in_refs = ((qf_ref, kf_ref, vf_ref, bef_ref, gcef_ref, gcrf_ref),
               (qb_ref, kb_ref, vb_ref, beb_ref, gceb_ref, gcrb_ref))
    kgp_refs = (kgpf_ref, kgpb_ref)
    o_refs = (of_ref, ob_ref)
    U, W, QG, QKM = range(4)

    def blockdiag(y):
        return jnp.concatenate([jnp.where(hm, y, 0.0) for hm in heads], axis=0).astype(BF16)

    def rows_of(ci):
        return slice(ci * cs, (ci + 1) * cs)

    loc = {}

    def local_scores(d, ci):
        q_ref, k_ref, v_ref, be_ref, gce_ref, gcr_ref = in_refs[d]
        rs = rows_of(ci)
        q = q_ref[rs, :]
        k = k_ref[rs, :]
        be = be_ref[0, rs, :]
        kb = k * be
        kq = _dot_nt(jnp.concatenate([kb, q], axis=0).astype(BF16), blockdiag(k))
        loc[d, ci] = dict(kq=kq, kb=kb)

    def local_decay(d, ci):
        st = loc[d, ci]
        _, _, _, _, gce_ref, gcr_ref = in_refs[d]
        rs = rows_of(ci)
        incl = (ti >= tj) if d == 0 else (ti <= tj)
        strict = (ti > tj) if d == 0 else (ti < tj)
        diff = gce_ref[0, rs, :] - gcr_ref[0, rs, :]
        decay = jnp.where(incl, jnp.exp(jnp.where(incl, diff, 0.0)), 0.0)
        kq = st.pop("kq")
        a = jnp.where(strict, kq[0:cs] * decay, 0.0)
        stage_ref[par, d, QKM, rs, :] = kq[cs:2 * cs] * decay
        st["pw"] = _dot(a.astype(BF16), blockdiag(a))
        st["t"] = eye - a

    def local_invert(d, ci, last):
        st = loc[d, ci]
        bdp = blockdiag(st["pw"])
        if last:
            st["t"] = st["t"] + _dot(st["t"].astype(BF16), bdp)
        else:
            both = _dot(jnp.concatenate([st["pw"], st["t"]], axis=0).astype(BF16), bdp)
            st["pw"] = both[0:cs]
            st["t"] = st["t"] + both[cs:2 * cs]

    def local_uw(d, ci):
        st = loc.pop((d, ci))
        q_ref, _, v_ref, be_ref, gce_ref, _ = in_refs[d]
        rs = rows_of(ci)
        eg = jnp.exp(gce_ref[0, rs, :])
        vb = v_ref[rs, :] * be_ref[0, rs, :]
        rhs = jnp.concatenate([blockdiag(vb), blockdiag(st["kb"] * eg)], axis=1)
        uw = _dot(st["t"].astype(BF16), rhs)
        stage_ref[par, d, U, rs, :] = uw[:, 0:w]
        stage_ref[par, d, W, rs, :] = uw[:, w:2 * w]
        stage_ref[par, d, QG, rs, :] = q_ref[rs, :] * eg
        edge = (ci + 1) * cs - 1 if d == 0 else ci * cs
        gl_ref[par, d, ci] = jnp.exp(gce_ref[0, edge:edge + 1, :])

    scan = {}

    def scan_read(d, ci):
        rs = rows_of(ci)
        lhs = jnp.concatenate([stage_ref[prv, d, W, rs, :], stage_ref[prv, d, QG, rs, :]], axis=0)
        scan[d] = _dot(lhs.astype(BF16), blockdiag(state_ref[d]))

    def scan_out(d, ci):
        rs = rows_of(ci)
        ws = scan[d]
        vn = stage_ref[prv, d, U, rs, :] - ws[0:cs]
        lhs = jnp.concatenate([stage_ref[prv, d, QKM, rs, :].astype(BF16), kgp_refs[d][0, rs, :]], axis=0)
        res = _dot(lhs, blockdiag(vn))
        o_refs[d][rs, :] = ws[cs:2 * cs] + res[0:cs]
        state_ref[d] = state_ref[d] * gl_ref[prv, d, ci] + res[cs:2 * cs]

    order = [(d, ci if d == 0 else chunks - 1 - ci) for ci in range(chunks) for d in range(N_DIR)]
    local_ops = []
    for w0 in range(0, len(order), wave):
        wv = order[w0:w0 + wave]
        local_ops += [functools.partial(local_scores, d, ci) for d, ci in wv]
        local_ops += [functools.partial(local_decay, d, ci) for d, ci in wv]
        for step in range(5):
            local_ops += [functools.partial(local_invert, d, ci, step == 4) for d, ci in wv]
        local_ops += [functools.partial(local_uw, d, ci) for d, ci in wv]
    scan_ops = []
    for ci in range(chunks):
        for fn in (scan_read, scan_out):
            scan_ops.append([functools.partial(fn, d, ci if d == 0 else chunks - 1 - ci) for d in range(N_DIR)])
    every = len(local_ops) // len(scan_ops)
    for idx, op in enumerate(local_ops):
        op()
        if (idx + 1) % every == 0 and scan_ops:
            for s_op in scan_ops.pop(0):
                s_op()
    for level in scan_ops:
        for s_op in level:
            s_op()


def _gdn_scan(q, k, v, be, gce, gcr, kgp, batch, seq, chunks):
    n_tok = q.shape[0]
    rows = chunks * GDN_CHUNK
    ng = seq // rows
    w = GDN_WIDTH

    def group(d, lag):
        def f(b, n):
            g = jnp.clip(n - lag, 0, ng - 1)
            return b * ng + (ng - 1 - g if d == 1 else g)
        return f

    def tok(d, lag):
        f = group(d, lag)
        return lambda b, n: (f(b, n), 0)

    def dtok(d, lag):
        f = group(d, lag)
        return lambda b, n: (d, f(b, n), 0)

    def local_specs(d):
        return [pl.BlockSpec((rows, w), tok(d, 0))] * 3 + [pl.BlockSpec((1, rows, w), dtok(d, 0))] * 3

    out = jax.ShapeDtypeStruct((n_tok, w), F32)
    return pl.pallas_call(
        functools.partial(_gdn_scan_kernel, chunks=chunks, wave=2 * chunks),
        grid=(batch, ng + 1),
        in_specs=local_specs(0) + local_specs(1) + [pl.BlockSpec((1, rows, w), dtok(d, 1)) for d in range(N_DIR)],
        out_specs=[pl.BlockSpec((rows, w), tok(d, 1)) for d in range(N_DIR)],
        out_shape=[out, out],
        scratch_shapes=[pltpu.VMEM((N_DIR, GDN_DK, w), F32),
                        pltpu.VMEM((2, N_DIR, 4, rows, w), F32),
                        pltpu.VMEM((2, N_DIR, chunks, 1, w), F32)],
        compiler_params=_params(("parallel", "arbitrary")),
        name="gdn_scan",
    )(q, k, v, be, gce, gcr, q, k, v, be, gce, gcr, kgp, kgp)


def _mixffn_kernel(yp_p, yp_m, yp_n, ya_p, ya_m, ya_n, of_p, of_m, of_n, ob_p, ob_m, ob_n,
                   z_p, z_m, z_n, h_p, h_m, h_n, gw_ref, bd_ref, wo_ref, postmix_ref,
                   prew_ref, wg_ref, wv_ref, cg_ref, cv_ref, wd_ref, postw_ref, o_ref, act_ref, *, tps, chunk):
    tm = h_m.shape[0]
    _, first, last = _seq_edges(tps)

    def ext(p_ref, m_ref, n_ref):
        p = p_ref[...].astype(F32)
        n = n_ref[...].astype(F32)
        return jnp.concatenate([p[p.shape[0] - HALO:], m_ref[...].astype(F32), n[0:HALO]], axis=0)

    a0 = POOL_WIDTH
    a1 = POOL_WIDTH + ATTN_WIDTH
    mix = _dot(ext(yp_p, yp_m, yp_n).astype(BF16), wo_ref[0:a0, :])
    mix = mix + _dot(ext(ya_p, ya_m, ya_n).astype(BF16), wo_ref[a0:a1, :])
    o = ext(of_p, of_m, of_n) + ext(ob_p, ob_m, ob_n)
    ms = _group_sum(o * o, bd_ref[...]) * (1.0 / GDN_DK)
    z = ext(z_p, z_m, z_n)
    yg = o * lax.rsqrt(ms + EPS) * gw_ref[...] * (z / (1.0 + jnp.exp(-z)))
    mix = mix + _dot(yg.astype(BF16), wo_ref[a1:D_MODEL, :])
    h1 = ext(h_p, h_m, h_n) + _rms(mix, postmix_ref[...])

    row = lax.broadcasted_iota(jnp.int32, (tm + 2 * HALO, 1), 0)
    outside = (first & (row < HALO)) | (last & (row >= HALO + tm))
    xn = jnp.where(outside, 0.0, _rms(h1, prew_ref[...])).astype(BF16)
    sl = slice(HALO, HALO + tm)

    def conv3(up, cw):
        out = _shift_rows(up, -1) * cw[0:1] + up * cw[1:2] + _shift_rows(up, 1) * cw[2:3]
        return out[sl]

    for c in range(D_FF // chunk):
        cs = slice(c * chunk, (c + 1) * chunk)
        gate = conv3(_dot(xn, wg_ref[:, cs]), cg_ref[:, cs])
        val = conv3(_dot(xn, wv_ref[:, cs]), cv_ref[:, cs])
        inner = 0.7978845608028654 * (gate + 0.044715 * (gate * gate * gate))
        act_ref[:, cs] = (0.5 * gate * (1.0 + jnp.tanh(inner)) * val).astype(BF16)
    ff = _dot(act_ref[...], wd_ref[...])
    o_ref[...] = h1[sl] + _rms(ff, postw_ref[...])


def _mixffn(y_pool, y_attn, o_f, o_b, z, h, gdn_nw, bd256, w_out, post_mix_w,
            pre_w, w_up, conv_w, w_down, post_w, seq, tm, chunk):
    n_tok = h.shape[0]
    const = lambda i: (0, 0)
    second = lambda i: (0, 1)
    resident = pl.Buffered(1)

    def halo(width, rows):
        per = tm // rows
        last = n_tok // rows - 1
        return [
            pl.BlockSpec((rows, width), lambda i: (jnp.maximum(i * per - 1, 0), 0)),
            pl.BlockSpec((tm, width), lambda i: (i, 0)),
            pl.BlockSpec((rows, width), lambda i: (jnp.minimum((i + 1) * per, last), 0)),
        ]

    bf16_rows = 2 * SUBLANES
    in_specs = (halo(POOL_WIDTH, bf16_rows) + halo(ATTN_WIDTH, bf16_rows) + halo(GDN_WIDTH, HALO) * 3
                + halo(D_MODEL, HALO) + [
        pl.BlockSpec((1, GDN_WIDTH), const),
        pl.BlockSpec((GDN_WIDTH, GDN_WIDTH), const),
        pl.BlockSpec((D_MODEL, D_MODEL), const, pipeline_mode=resident),
        pl.BlockSpec((1, D_MODEL), const),
        pl.BlockSpec((1, D_MODEL), const),
        pl.BlockSpec((D_MODEL, D_FF), const, pipeline_mode=resident),
        pl.BlockSpec((D_MODEL, D_FF), second, pipeline_mode=resident),
        pl.BlockSpec((FFN_CONV, D_FF), const),
        pl.BlockSpec((FFN_CONV, D_FF), second),
        pl.BlockSpec((D_FF, D_MODEL), const, pipeline_mode=resident),
        pl.BlockSpec((1, D_MODEL), const),
    ])
    return pl.pallas_call(
        functools.partial(_mixffn_kernel, tps=seq // tm, chunk=chunk),
        grid=(n_tok // tm,),
        in_specs=in_specs,
        out_specs=pl.BlockSpec((tm, D_MODEL), lambda i: (i, 0)),
        out_shape=jax.ShapeDtypeStruct((n_tok, D_MODEL), F32),
        scratch_shapes=[pltpu.VMEM((tm, D_FF), BF16)],
        compiler_params=_params(("parallel",)),
        name="mixffn",
    )(y_pool, y_pool, y_pool, y_attn, y_attn, y_attn, o_f, o_f, o_f, o_b, o_b, o_b, z, z, z, h, h, h,
      gdn_nw, bd256, w_out, post_mix_w, pre_w, w_up, w_up, conv_w, conv_w, w_down, post_w)


def _rope_tables(seq):
    t = np.arange(seq)
    pos = np.stack([t // GRID_W, t % GRID_W], axis=-1).astype(np.float32)
    axis_dim = HEAD_DIM // 2
    inv_freq = (ROPE_THETA ** (-np.arange(0, axis_dim, 2, dtype=np.float32) / axis_dim)).astype(np.float32)
    ang = pos[:, :, None] * inv_freq
    j = np.arange(LANES) % HEAD_DIM
    a = ang[:, j // 32, j % 16]
    sign = np.where((j % 32) < 16, -1.0, 1.0).astype(np.float32)
    return jnp.asarray(np.cos(a), F32), jnp.asarray(np.sin(a) * sign, F32)


def _head_blockdiag(n):
    i = np.arange(n) // HEAD_DIM
    return jnp.asarray(i[:, None] == i[None, :], BF16)


def _gate_expand():
    e = np.zeros((LANES, 4 * GDN_WIDTH), np.float32)
    for kind in range(2):
        for d in range(N_DIR):
            for hh in range(GDN_HEADS):
                src = kind * N_DIR * GDN_HEADS + d * GDN_HEADS + hh
                dst = (kind * N_DIR + d) * GDN_WIDTH + hh * GDN_DK
                e[src, dst:dst + GDN_DK] = 1.0
    return jnp.asarray(e, BF16)


def _pad_lanes(v, offset):
    out = jnp.zeros((1, LANES), F32)
    return out.at[0, offset:offset + v.shape[0]].set(v.astype(F32))


def kernel(x, pre_mix_norm, post_mix_norm, pre_ffn_norm, post_ffn_norm, w_in, pool_w, pool_scale, q_norm, k_norm, gdn_conv, gdn_a_log, gdn_dt_bias, gdn_norm, w_out, w_up, ffn_conv, w_down):
    batch, seq, _ = x.shape
    n_tok = batch * seq
    depth = w_in.shape[0]
    assert seq % 512 == 0 and seq % GRID_W == 0

    tm_proj = 512
    tm_pool = 512
    tm_gdn = 256
    tm_ffn = 512
    cos_t, sin_t = _rope_tables(seq)
    bd256 = _head_blockdiag(GDN_WIDTH)
    expand = _gate_expand()
    n_gate = N_DIR * GDN_HEADS

    h = x.reshape(n_tok, D_MODEL)
    for l in range(depth):
        w_cat = jnp.pad(w_in[l], ((0, 0), (0, N_CAT - D_IN))).astype(BF16)
        qw = jnp.tile(q_norm[l], 2)[None, :]
        kw = jnp.tile(k_norm[l], 2)[None, :]
        u_pool, qt, k4, vt, gqkv, z, gates = _inproj(
            h, pre_mix_norm[l][None, :], w_cat, cos_t, sin_t, qw, kw, bd256, seq, tm_proj)

        bdw = jax.scipy.linalg.block_diag(*[pool_w[l, g] for g in range(POOL_GROUPS)]).astype(BF16)
        y_pool = _pool(u_pool, bdw, pool_scale[l][None, :], seq, tm_pool)

        y_attn = _attention(qt, k4, vt, batch, seq, 256, 256, 16)

        neg_a = _pad_lanes(-jnp.exp(gdn_a_log[l].reshape(-1)), n_gate)
        dt_b = _pad_lanes(gdn_dt_bias[l].reshape(-1), n_gate)
        gq, gk, gv, be, gce, gcr, kgt = _gdn_prep(gqkv, gates, gdn_conv[l], neg_a, dt_b, bd256, expand, seq, tm_gdn)
        o_f, o_b = _gdn_scan(gq, gk, gv, be, gce, gcr, kgt, batch, seq, 8)

        h = _mixffn(y_pool, y_attn, o_f, o_b, z, h, jnp.tile(gdn_norm[l], GDN_HEADS)[None, :], bd256,
                    w_out[l].astype(BF16), post_mix_norm[l][None, :],
                    pre_ffn_norm[l][None, :], w_up[l].astype(BF16), ffn_conv[l],
                    w_down[l].astype(BF16), post_ffn_norm[l][None, :], seq, tm_ffn, 256)
    return h.reshape(batch, seq, D_MODEL)
```

```python
import functools
import math

import numpy as np
import jax
import jax.numpy as jnp
from jax import lax
from jax.experimental import pallas as pl
from jax.experimental.pallas import tpu as pltpu

F32 = jnp.float32
BF16 = jnp.bfloat16

D_MODEL = 1024
GRID_W = 64
EPS = 1e-6
POOL_WIDTH = 256
POOL_WINDOWS = (2, 4, 8, 16)
POOL_GROUPS = 4
POOL_GDIM = POOL_WIDTH // POOL_GROUPS
ATTN_HEADS = 8
ATTN_KV_HEADS = 2
HEAD_DIM = 64
ATTN_WIDTH = ATTN_HEADS * HEAD_DIM
KV_WIDTH = ATTN_KV_HEADS * HEAD_DIM
ROPE_THETA = 10000.0
GDN_HEADS = 4
GDN_DK = 64
GDN_WIDTH = GDN_HEADS * GDN_DK
GDN_CONV = 5
GDN_CHUNK = 64
N_DIR = 2
D_FF = 2816
FFN_CONV = 3
D_IN = POOL_WIDTH + ATTN_WIDTH + 2 * KV_WIDTH + 4 * GDN_WIDTH + 2 * N_DIR * GDN_HEADS

LANES = 128
SUBLANES = 8
HALO = SUBLANES
VMEM_LIMIT = 56 * 1024 * 1024

N_CAT = 2176
COL_POOL = 0
COL_QA = COL_POOL + POOL_WIDTH
COL_KA = COL_QA + ATTN_WIDTH
COL_VA = COL_KA + KV_WIDTH
COL_GQKV = COL_VA + KV_WIDTH
COL_Z = COL_GQKV + 3 * GDN_WIDTH
COL_GATES = COL_Z + GDN_WIDTH

Q_SCALE = HEAD_DIM ** -0.5 * math.log2(math.e)
V_ROWS = HEAD_DIM + 16


def _dot(a, b):
    return jnp.dot(a, b, preferred_element_type=F32)


def _dot_nt(a, b):
    return lax.dot_general(a, b, (((1,), (1,)), ((), ())), preferred_element_type=F32)


def _split2(x):
    hi = x.astype(BF16)
    lo = (x - hi.astype(F32)).astype(BF16)
    return hi, lo


def _sel_dot(m01, x):
    hi, lo = _split2(x)
    return _dot(m01, hi) + _dot(m01, lo)


def _dot_sel(x, m01):
    hi, lo = _split2(x)
    return _dot(hi, m01) + _dot(lo, m01)


def _group_sum(x, bd):
    return _dot_sel(x, bd)


def _rms(x, w):
    ms = jnp.mean(x * x, axis=-1, keepdims=True)
    return x * lax.rsqrt(ms + EPS) * w


def _shift_rows(x, d):
    n = x.shape[0]
    return pltpu.roll(x, (-d) % n, 0)


def _params(sem):
    return pltpu.CompilerParams(dimension_semantics=sem, vmem_limit_bytes=VMEM_LIMIT)


def _halo_specs(tm, width, n_rows):
    per = tm // HALO
    last = n_rows // HALO - 1
    return [
        pl.BlockSpec((HALO, width), lambda i: (jnp.maximum(i * per - 1, 0), 0)),
        pl.BlockSpec((tm, width), lambda i: (i, 0)),
        pl.BlockSpec((HALO, width), lambda i: (jnp.minimum((i + 1) * per, last), 0)),
    ]


def _seq_edges(tiles_per_seq):
    t = pl.program_id(0) % tiles_per_seq
    return t, t == 0, t == tiles_per_seq - 1


def _inproj_kernel(xp_ref, x_ref, xnx_ref, nw_ref, w_ref, cos_ref, sin_ref, qw_ref, kw_ref, bd_ref,
                   bdw_ref, psc_ref, yp_ref, q_ref, k_ref, vt_ref, g_ref, z_ref, gt_ref, *, tps, seq):
    tm = x_ref.shape[0]
    xn = _rms(x_ref[...], nw_ref[...]).astype(BF16)

    cos = cos_ref[...]
    sin = sin_ref[...]
    lane = lax.broadcasted_iota(jnp.int32, (tm, LANES), 1)
    first_half = (lane & 16) == 0
    lo64 = lane < HEAD_DIM
    bd = bd_ref[...]

    def head_norm_rope(x, ss, w):
        y = x * lax.rsqrt(ss * (1.0 / HEAD_DIM) + EPS) * w
        partner = jnp.where(first_half, pltpu.roll(y, LANES - 16, 1), pltpu.roll(y, 16, 1))
        return y * cos + partner * sin

    n_slab = ATTN_WIDTH // LANES
    qa = _dot(xn, w_ref[:, COL_QA:COL_KA])
    ka = _dot(xn, w_ref[:, COL_KA:COL_VA])
    ss_q = [_group_sum(qa[:, p * 2 * LANES:(p + 1) * 2 * LANES] ** 2, bd) for p in range(n_slab // 2)]
    ss_q = [ss_q[s // 2][:, (s % 2) * LANES:(s % 2 + 1) * LANES] for s in range(n_slab)]
    qa = [qa[:, s * LANES:(s + 1) * LANES] for s in range(n_slab)]
    ss_k = _group_sum(ka * ka, bd[0:LANES, 0:LANES])

    t, first, last = _seq_edges(tps)
    w_pool = w_ref[:, COL_POOL:COL_QA]
    u = _dot(xn, w_pool)
    xh = _rms(jnp.concatenate([xp_ref[...], xnx_ref[...]], axis=0), nw_ref[...]).astype(BF16)
    uh = _dot(xh, w_pool)
    ext = jnp.concatenate([jnp.where(first, 0.0, uh[0:HALO]), u, jnp.where(last, 0.0, uh[HALO:2 * HALO])], axis=0)
    w2 = ext + _shift_rows(ext, -1)
    w4 = _shift_rows(w2, -1) + _shift_rows(w2, 1)
    w8 = _shift_rows(w4, -2) + _shift_rows(w4, 2)
    w16 = _shift_rows(w8, -4) + _shift_rows(w8, 4)
    sl = slice(HALO, HALO + tm)
    grp = lax.broadcasted_iota(jnp.int32, (tm, POOL_WIDTH), 1) // POOL_GDIM
    seg = jnp.where(grp == 0, w2[sl], jnp.where(grp == 1, w4[sl], jnp.where(grp == 2, w8[sl], w16[sl])))
    half = jnp.where(grp == 0, 1, jnp.where(grp == 1, 2, jnp.where(grp == 2, 4, 8)))
    pos = t * tm + lax.broadcasted_iota(jnp.int32, (tm, POOL_WIDTH), 0)
    cnt = (jnp.minimum(pos + half, seq) - jnp.maximum(pos - half, 0)).astype(F32)
    pooled = seg / cnt - u
    yp_ref[...] = (_dot(pooled.astype(BF16), bdw_ref[...]) * psc_ref[...]).astype(BF16)

    qw = qw_ref[...] * Q_SCALE
    q_rot = jnp.concatenate([head_norm_rope(qa[s], ss_q[s], qw) for s in range(n_slab)], axis=1)
    q_ref[...] = q_rot.T.astype(BF16)

    kr = head_norm_rope(ka, ss_k, kw_ref[...])
    ks = pltpu.roll(kr, HEAD_DIM, 1)
    k_ref[:, 0:128] = jnp.where(lo64, kr, 0.0).astype(BF16)
    k_ref[:, 128:256] = jnp.where(lo64, 0.0, ks).astype(BF16)
    k_ref[:, 256:384] = jnp.where(lo64, ks, 0.0).astype(BF16)
    k_ref[:, 384:512] = jnp.where(lo64, 0.0, kr).astype(BF16)

    vat = _dot(xn, w_ref[:, COL_VA:COL_GQKV]).T
    pad_row = lax.broadcasted_iota(jnp.int32, (V_ROWS - HEAD_DIM, tm), 0)
    ones_rows = jnp.where(pad_row == 0, 1.0, 0.0)
    vt_ref[...] = jnp.concatenate(
        [vat[0:HEAD_DIM], ones_rows, vat[HEAD_DIM:2 * HEAD_DIM], ones_rows], axis=0).astype(BF16)
    g_ref[...] = _dot(xn, w_ref[:, COL_GQKV:COL_Z])
    z_ref[...] = _dot(xn, w_ref[:, COL_Z:COL_GATES])
    gt_ref[...] = _dot(xn, w_ref[:, COL_GATES:N_CAT])


def _inproj(x2d, norm_w, w_cat, cos_t, sin_t, qw, kw, bd256, bdw, pool_scale, seq, tm, layer):
    n_tok = x2d.shape[0]
    tps = seq // tm
    const = lambda i: (0, 0)
    row = lambda i: (i, 0)
    pos = lambda i: (i % tps, 0)
    return pl.pallas_call(
        functools.partial(_inproj_kernel, tps=tps, seq=seq),
        grid=(n_tok // tm,),
        in_specs=_halo_specs(tm, D_MODEL, n_tok) + [
            pl.BlockSpec((None, 1, D_MODEL), lambda i: (layer, 0, 0)),
            pl.BlockSpec((None, D_MODEL, N_CAT), lambda i: (layer, 0, 0)),
            pl.BlockSpec((tm, LANES), pos),
            pl.BlockSpec((tm, LANES), pos),
            pl.BlockSpec((1, LANES), const),
            pl.BlockSpec((1, LANES), const),
            pl.BlockSpec((2 * LANES, 2 * LANES), const),
            pl.BlockSpec((POOL_WIDTH, POOL_WIDTH), const),
            pl.BlockSpec((1, POOL_WIDTH), const),
        ],
        out_specs=[
            pl.BlockSpec((tm, POOL_WIDTH), row),
            pl.BlockSpec((ATTN_WIDTH, tm), lambda i: (0, i)),
            pl.BlockSpec((tm, 4 * LANES), row),
            pl.BlockSpec((ATTN_KV_HEADS * V_ROWS, tm), lambda i: (0, i)),
            pl.BlockSpec((tm, 3 * GDN_WIDTH), row),
            pl.BlockSpec((tm, GDN_WIDTH), row),
            pl.BlockSpec((tm, LANES), row),
        ],
        out_shape=[
            jax.ShapeDtypeStruct((n_tok, POOL_WIDTH), BF16),
            jax.ShapeDtypeStruct((ATTN_WIDTH, n_tok), BF16),
            jax.ShapeDtypeStruct((n_tok, 4 * LANES), BF16),
            jax.ShapeDtypeStruct((ATTN_KV_HEADS * V_ROWS, n_tok), BF16),
            jax.ShapeDtypeStruct((n_tok, 3 * GDN_WIDTH), F32),
            jax.ShapeDtypeStruct((n_tok, GDN_WIDTH), F32),
            jax.ShapeDtypeStruct((n_tok, LANES), F32),
        ],
        compiler_params=_params(("parallel",)),
        name="inproj",
    )(x2d, x2d, x2d, norm_w, w_cat, cos_t, sin_t, qw, kw, bd256, bdw, pool_scale)


def _attn_kernel(q_ref, qn_ref, k_ref, vt_ref, o_ref, acc_ref, m_ref, s_ref, bm_ref, qq_ref, *, tk, per_iter):
    seq = k_ref.shape[0]
    group = ATTN_HEADS // ATTN_KV_HEADS
    n_chunks = seq // tk
    n_iter = n_chunks // per_iter
    acc_ref[...] = jnp.zeros_like(acc_ref)
    m_ref[...] = jnp.full_like(m_ref, -1e30)
    qq_ref[0] = q_ref[...]
    qq_ref[1] = qn_ref[...]

    def scores(c, buf, slab, tile=0):
        r0 = pl.multiple_of(c * tk, tk)
        kk = jnp.concatenate([k_ref[pl.ds(r0, tk), 0:LANES], k_ref[pl.ds(r0, tk), LANES:2 * LANES]], axis=0)
        s2 = _dot(kk, qq_ref[tile, slab * LANES:(slab + 1) * LANES, :])
        for half in range(2):
            j = 2 * slab + half
            s = s2[half * tk:(half + 1) * tk]
            s_ref[buf, j] = s
            bm_ref[buf, j] = jnp.max(s, axis=0, keepdims=True)

    def softmax_pv(c, buf, slab):
        r0 = pl.multiple_of(c * tk, tk)
        vt = vt_ref[:, pl.ds(r0, tk)]
        for j in (2 * slab, 2 * slab + 1):
            m = m_ref[j]
            m_new = jnp.maximum(m, bm_ref[buf, j])
            alpha = jnp.exp2(m - m_new)
            p = jnp.exp2(s_ref[buf, j] - m_new)
            m_ref[j] = m_new
            acc_ref[j] = alpha * acc_ref[j] + _dot(vt, p.astype(BF16))

    slabs = range(group // 2)

    @pl.when(pl.program_id(2) == 0)
    def _():
        for slab in slabs:
            scores(0, 0, slab)

    def body(i, carry):
        c0 = per_iter * i
        wrap = (i == n_iter - 1).astype(jnp.int32)
        for u in range(per_iter):
            for slab in slabs:
                if u == per_iter - 1:
                    scores((1 - wrap) * (c0 + per_iter), 0, slab, tile=wrap)
                else:
                    scores(c0 + u + 1, (u + 1) % 2, slab)
            for slab in slabs:
                softmax_pv(c0 + u, u % 2, slab)
        return carry

    lax.fori_loop(0, n_iter, body, 0)
    out = jnp.concatenate([acc_ref[j, 0:HEAD_DIM] / acc_ref[j, HEAD_DIM:HEAD_DIM + 1] for j in range(group)], axis=0)
    o_ref[...] = out.T.astype(BF16)


def _attention(qt, k4, vt, batch, seq, tq, tk, per_iter):
    n_tok = k4.shape[0]
    nq = seq // tq
    group = ATTN_HEADS // ATTN_KV_HEADS
    gw = group * HEAD_DIM
    per_iter = min(per_iter, seq // tk)
    assert per_iter % 2 == 0 and (seq // tk) % per_iter == 0
    return pl.pallas_call(
        functools.partial(_attn_kernel, tk=tk, per_iter=per_iter),
        grid=(batch, ATTN_KV_HEADS, nq),
        in_specs=[
            pl.BlockSpec((gw, tq), lambda b, g, i: (g, b * nq + i)),
            pl.BlockSpec((gw, tq), lambda b, g, i: (g, b * nq + jnp.minimum(i + 1, nq - 1))),
            pl.BlockSpec((seq, 2 * LANES), lambda b, g, i: (b, g)),
            pl.BlockSpec((V_ROWS, seq), lambda b, g, i: (g, b)),
        ],
        out_specs=pl.BlockSpec((tq, gw), lambda b, g, i: (b * nq + i, g)),
        out_shape=jax.ShapeDtypeStruct((n_tok, ATTN_WIDTH), BF16),
        scratch_shapes=[pltpu.VMEM((group, V_ROWS, tq), F32),
                        pltpu.VMEM((group, 1, tq), F32),
                        pltpu.VMEM((2, group, tk, tq), F32),
                        pltpu.VMEM((2, group, 1, tq), F32),
                        pltpu.VMEM((2, gw, tq), BF16)],
        compiler_params=_params(("parallel", "parallel", "arbitrary")),
        name="attention",
    )(qt, qt, k4, vt)


def _gdn_prep_kernel(prev_ref, x_ref, next_ref, gt_ref, cw_ref, na_ref, dtb_ref, bd_ref, ex_ref,
                     q_ref, k_ref, v_ref, be_ref, gce_ref, gcr_ref, kgt_ref, *, tps):
    tm = x_ref.shape[0]
    _, first, last = _seq_edges(tps)
    prev = jnp.where(first, 0.0, prev_ref[...])
    nxt = jnp.where(last, 0.0, next_ref[...])
    ext = jnp.concatenate([prev, x_ref[...], nxt], axis=0)
    cw = cw_ref[...]
    conv = ext * cw[2:3]
    for kk in (0, 1, 3, 4):
        conv = conv + _shift_rows(ext, kk - GDN_CONV // 2) * cw[kk:kk + 1]
    conv = conv[HALO:HALO + tm]
    act = conv / (1.0 + jnp.exp(-conv))
    bd = bd_ref[...]

    def l2n(x):
        return x * lax.rsqrt(_group_sum(x * x, bd) + EPS)

    q = l2n(act[:, 0:GDN_WIDTH]) * (GDN_DK ** -0.5)
    k = l2n(act[:, GDN_WIDTH:2 * GDN_WIDTH])
    q_ref[...] = q
    k_ref[...] = k
    v_ref[...] = act[:, 2 * GDN_WIDTH:3 * GDN_WIDTH]

    gt = gt_ref[...]
    lane = lax.broadcasted_iota(jnp.int32, (tm, LANES), 1)
    beta = 1.0 / (1.0 + jnp.exp(-gt))
    xs = gt + dtb_ref[...]
    softplus = jnp.maximum(xs, 0.0) + jnp.log1p(jnp.exp(-jnp.abs(xs)))
    g = na_ref[...] * softplus

    r = lax.broadcasted_iota(jnp.int32, (tm, tm), 0)
    c = lax.broadcasted_iota(jnp.int32, (tm, tm), 1)
    same = (r // GDN_CHUNK) == (c // GDN_CHUNK)
    ones_bd = jnp.where(same, 1.0, 0.0).astype(BF16)
    tri_f = jnp.where(same & (c <= r), 1.0, 0.0).astype(BF16)
    tri_b = jnp.where(same & (c >= r), 1.0, 0.0).astype(BF16)
    gc = jnp.where(lane < 12, _sel_dot(tri_f, g), _sel_dot(tri_b, g))
    comp = jnp.where(lane < 8, beta, gc)
    exp_all = _dot_sel(comp, ex_ref[...])
    tot = _sel_dot(ones_bd, g)
    tot_e = _dot_sel(tot, ex_ref[:, 2 * GDN_WIDTH:4 * GDN_WIDTH])

    row = lax.broadcasted_iota(jnp.int32, (tm, GDN_WIDTH), 0)
    col = lax.broadcasted_iota(jnp.int32, (tm, GDN_WIDTH), 1)
    diag = (row % GDN_CHUNK) == (col % GDN_DK)
    for d in range(N_DIR):
        be_ref[d] = exp_all[:, d * GDN_WIDTH:(d + 1) * GDN_WIDTH]
        gce = exp_all[:, (2 + d) * GDN_WIDTH:(3 + d) * GDN_WIDTH]
        gce_ref[d] = gce
        gcr_ref[d] = _sel_dot(ones_bd, jnp.where(diag, gce, 0.0))
        kg = k * jnp.exp(tot_e[:, d * GDN_WIDTH:(d + 1) * GDN_WIDTH] - gce)
        lo_lane = lax.broadcasted_iota(jnp.int32, (GDN_DK, LANES), 1) < GDN_DK
        for pr in range(tm // LANES):
            xt = kg[pr * LANES:(pr + 1) * LANES, :].T
            xr = pltpu.roll(xt, GDN_CHUNK, 1)
            for cl in range(2):
                even, odd = (xt, xr) if cl == 0 else (xr, xt)
                tiles = [jnp.where(lo_lane, even[(2 * p) * GDN_DK:(2 * p + 1) * GDN_DK],
                                   odd[(2 * p + 1) * GDN_DK:(2 * p + 2) * GDN_DK]) for p in range(GDN_HEADS // 2)]
                r0 = (2 * pr + cl) * GDN_CHUNK
                kgt_ref[d, r0:r0 + GDN_CHUNK, :] = jnp.concatenate(tiles, axis=1).astype(BF16)


def _gdn_prep(gqkv, gates, conv_w, neg_a, dt_b, bd256, expand, seq, tm):
    n_tok = gqkv.shape[0]
    width = 3 * GDN_WIDTH
    const = lambda i: (0, 0)
    row = lambda i: (i, 0)
    drow = lambda i: (0, i, 0)
    tok = jax.ShapeDtypeStruct((n_tok, GDN_WIDTH), F32)
    dtok = jax.ShapeDtypeStruct((N_DIR, n_tok, GDN_WIDTH), F32)
    return pl.pallas_call(
        functools.partial(_gdn_prep_kernel, tps=seq // tm),
        grid=(n_tok // tm,),
        in_specs=_halo_specs(tm, width, n_tok) + [
            pl.BlockSpec((tm, LANES), row),
            pl.BlockSpec((GDN_CONV, width), const),
            pl.BlockSpec((1, LANES), const),
            pl.BlockSpec((1, LANES), const),
            pl.BlockSpec((GDN_WIDTH, GDN_WIDTH), const),
            pl.BlockSpec((LANES, 4 * GDN_WIDTH), const),
        ],
        out_specs=[
            pl.BlockSpec((tm, GDN_WIDTH), row),
            pl.BlockSpec((tm, GDN_WIDTH), row),
            pl.BlockSpec((tm, GDN_WIDTH), row),
            pl.BlockSpec((N_DIR, tm, GDN_WIDTH), drow),
            pl.BlockSpec((N_DIR, tm, GDN_WIDTH), drow),
            pl.BlockSpec((N_DIR, tm, GDN_WIDTH), drow),
            pl.BlockSpec((N_DIR, tm, GDN_WIDTH), drow),
        ],
        out_shape=[tok, tok, tok, dtok, dtok, dtok,
                   jax.ShapeDtypeStruct((N_DIR, n_tok, GDN_WIDTH), BF16)],
        compiler_params=_params(("parallel",)),
        name="gdn_prep",
    )(gqkv, gqkv, gqkv, gates, conv_w, neg_a, dt_b, bd256, expand)


def _gdn_scan_kernel(qf_ref, kf_ref, vf_ref, bef_ref, gcef_ref, gcrf_ref,
                     qb_ref, kb_ref, vb_ref, beb_ref, gceb_ref, gcrb_ref, kgpf_ref, kgpb_ref,
                     of_ref, ob_ref, state_ref, stage_ref, gl_ref, *, chunks, wave):
    cs = GDN_CHUNK
    w = GDN_WIDTH
    par = pl.program_id(1) % 2
    prv = 1 - par

    @pl.when(pl.program_id(1) == 0)
    def _():
        state_ref[...] = jnp.zeros_like(state_ref)
        stage_ref[...] = jnp.zeros_like(stage_ref)
        gl_ref[...] = jnp.zeros_like(gl_ref)

    lane_h = lax.broadcasted_iota(jnp.int32, (cs, w), 1) // GDN_DK
    heads = [lane_h == h for h in range(GDN_HEADS)]
    ti = lax.broadcasted_iota(jnp.int32, (cs, w), 0)
    tj = lax.broadcasted_iota(jnp.int32, (cs, w), 1) % cs
    eye = (ti == tj).astype(F32)
    in_refs = ((qf_ref, kf_ref, vf_ref, bef_ref, gcef_ref, gcrf_ref),
               (qb_ref, kb_ref, vb_ref, beb_ref, gceb_ref, gcrb_ref))
    kgp_refs = (kgpf_ref, kgpb_ref)
    o_refs = (of_ref, ob_ref)
    U, W, QG, QKM = range(4)

    def blockdiag(y):
        return jnp.concatenate([jnp.where(hm, y, 0.0) for hm in heads], axis=0).astype(BF16)

    def rows_of(ci):
        return slice(ci * cs, (ci + 1) * cs)

    loc = {}

    def local_scores(d, ci):
        q_ref, k_ref, v_ref, be_ref, gce_ref, gcr_ref = in_refs[d]
        rs = rows_of(ci)
        q = q_ref[rs, :]
        k = k_ref[rs, :]
        be = be_ref[0, rs, :]
        kb = k * be
        kq = _dot_nt(jnp.concatenate([kb, q], axis=0).astype(BF16), blockdiag(k))
        loc[d, ci] = dict(kq=kq, kb=kb)

    def local_decay(d, ci):
        st = loc[d, ci]
        _, _, _, _, gce_ref, gcr_ref = in_refs[d]
        rs = rows_of(ci)
        incl = (ti >= tj) if d == 0 else (ti <= tj)
        strict = (ti > tj) if d == 0 else (ti < tj)
        diff = gce_ref[0, rs, :] - gcr_ref[0, rs, :]
        decay = jnp.where(incl, jnp.exp(jnp.where(incl, diff, 0.0)), 0.0)
        kq = st.pop("kq")
        a = jnp.where(strict, kq[0:cs] * decay, 0.0)
        stage_ref[par, d, QKM, rs, :] = kq[cs:2 * cs] * decay
        st["pw"] = _dot(a.astype(BF16), blockdiag(a))
        st["t"] = eye - a

    def local_invert(d, ci, last):
        st = loc[d, ci]
        bdp = blockdiag(st["pw"])
        if last:
            st["t"] = st["t"] + _dot(st["t"].astype(BF16), bdp)
        else:
            both = _dot(jnp.concatenate([st["pw"], st["t"]], axis=0).astype(BF16), bdp)
            st["pw"] = both[0:cs]
            st["t"] = st["t"] + both[cs:2 * cs]

    def local_uw(d, ci):
        st = loc.pop((d, ci))
        q_ref, _, v_ref, be_ref, gce_ref, _ = in_refs[d]
        rs = rows_of(ci)
        eg = jnp.exp(gce_ref[0, rs, :])
        vb = v_ref[rs, :] * be_ref[0, rs, :]
        rhs = jnp.concatenate([blockdiag(vb), blockdiag(st["kb"] * eg)], axis=1)
        uw = _dot(st["t"].astype(BF16), rhs)
        stage_ref[par, d, U, rs, :] = uw[:, 0:w]
        stage_ref[par, d, W, rs, :] = uw[:, w:2 * w]
        stage_ref[par, d, QG, rs, :] = q_ref[rs, :] * eg
        edge = (ci + 1) * cs - 1 if d == 0 else ci * cs
        gl_ref[par, d, ci] = jnp.exp(gce_ref[0, edge:edge + 1, :])

    scan = {}

    def scan_read(d, ci):
        rs = rows_of(ci)
        lhs = jnp.concatenate([stage_ref[prv, d, W, rs, :], stage_ref[prv, d, QG, rs, :]], axis=0)
        scan[d] = _dot(lhs.astype(BF16), blockdiag(state_ref[d]))

    def scan_out(d, ci):
        rs = rows_of(ci)
        ws = scan[d]
        vn = stage_ref[prv, d, U, rs, :] - ws[0:cs]
        lhs = jnp.concatenate([stage_ref[prv, d, QKM, rs, :].astype(BF16), kgp_refs[d][0, rs, :]], axis=0)
        res = _dot(lhs, blockdiag(vn))
        o_refs[d][rs, :] = ws[cs:2 * cs] + res[0:cs]
        state_ref[d] = state_ref[d] * gl_ref[prv, d, ci] + res[cs:2 * cs]

    order = [(d, ci if d == 0 else chunks - 1 - ci) for ci in range(chunks) for d in range(N_DIR)]
    local_ops = []
    for w0 in range(0, len(order), wave):
        wv = order[w0:w0 + wave]
        local_ops += [functools.partial(local_scores, d, ci) for d, ci in wv]
        local_ops += [functools.partial(local_decay, d, ci) for d, ci in wv]
        for step in range(5):
            local_ops += [functools.partial(local_invert, d, ci, step == 4) for d, ci in wv]
        local_ops += [functools.partial(local_uw, d, ci) for d, ci in wv]
    scan_ops = []
    for ci in range(chunks):
        for fn in (scan_read, scan_out):
            scan_ops.append([functools.partial(fn, d, ci if d == 0 else chunks - 1 - ci) for d in range(N_DIR)])
    every = len(local_ops) // len(scan_ops)
    for idx, op in enumerate(local_ops):
        op()
        if (idx + 1) % every == 0 and scan_ops:
            for s_op in scan_ops.pop(0):
                s_op()
    for level in scan_ops:
        for s_op in level:
            s_op()


def _gdn_scan(q, k, v, be, gce, gcr, kgp, batch, seq, chunks):
    n_tok = q.shape[0]
    rows = chunks * GDN_CHUNK
    ng = seq // rows
    w = GDN_WIDTH

    def group(d, lag):
        def f(b, n):
            g = jnp.clip(n - lag, 0, ng - 1)
            return b * ng + (ng - 1 - g if d == 1 else g)
        return f

    def tok(d, lag):
        f = group(d, lag)
        return lambda b, n: (f(b, n), 0)

    def dtok(d, lag):
        f = group(d, lag)
        return lambda b, n: (d, f(b, n), 0)

    def local_specs(d):
        return [pl.BlockSpec((rows, w), tok(d, 0))] * 3 + [pl.BlockSpec((1, rows, w), dtok(d, 0))] * 3

    out = jax.ShapeDtypeStruct((n_tok, w), F32)
    return pl.pallas_call(
        functools.partial(_gdn_scan_kernel, chunks=chunks, wave=2 * chunks),
        grid=(batch, ng + 1),
        in_specs=local_specs(0) + local_specs(1) + [pl.BlockSpec((1, rows, w), dtok(d, 1)) for d in range(N_DIR)],
        out_specs=[pl.BlockSpec((rows, w), tok(d, 1)) for d in range(N_DIR)],
        out_shape=[out, out],
        scratch_shapes=[pltpu.VMEM((N_DIR, GDN_DK, w), F32),
                        pltpu.VMEM((2, N_DIR, 4, rows, w), F32),
                        pltpu.VMEM((2, N_DIR, chunks, 1, w), F32)],
        compiler_params=_params(("parallel", "arbitrary")),
        name="gdn_scan",
    )(q, k, v, be, gce, gcr, q, k, v, be, gce, gcr, kgp, kgp)


def _mixffn_kernel(yp_p, yp_m, yp_n, ya_p, ya_m, ya_n, of_p, of_m, of_n, ob_p, ob_m, ob_n,
                   z_p, z_m, z_n, h_p, h_m, h_n, gw_ref, bd_ref, wo_ref, postmix_ref,
                   prew_ref, wg_ref, wv_ref, cg_ref, cv_ref, wd_ref, postw_ref, o_ref, act_ref, *, tps, chunk):
    tm = h_m.shape[0]
    _, first, last = _seq_edges(tps)

    def ext(p_ref, m_ref, n_ref):
        p = p_ref[...].astype(F32)
        n = n_ref[...].astype(F32)
        return jnp.concatenate([p[p.shape[0] - HALO:], m_ref[...].astype(F32), n[0:HALO]], axis=0)

    a0 = POOL_WIDTH
    a1 = POOL_WIDTH + ATTN_WIDTH
    mix = _dot(ext(yp_p, yp_m, yp_n).astype(BF16), wo_ref[0:a0, :])
    mix = mix + _dot(ext(ya_p, ya_m, ya_n).astype(BF16), wo_ref[a0:a1, :])
    o = ext(of_p, of_m, of_n) + ext(ob_p, ob_m, ob_n)
    ms = _group_sum(o * o, bd_ref[...]) * (1.0 / GDN_DK)
    z = ext(z_p, z_m, z_n)
    yg = o * lax.rsqrt(ms + EPS) * gw_ref[...] * (z / (1.0 + jnp.exp(-z)))
    mix = mix + _dot(yg.astype(BF16), wo_ref[a1:D_MODEL, :])
    h1 = ext(h_p, h_m, h_n) + _rms(mix, postmix_ref[...])

    row = lax.broadcasted_iota(jnp.int32, (tm + 2 * HALO, 1), 0)
    outside = (first & (row < HALO)) | (last & (row >= HALO + tm))
    xn = jnp.where(outside, 0.0, _rms(h1, prew_ref[...])).astype(BF16)
    sl = slice(HALO, HALO + tm)

    def conv3(up, cw):
        out = _shift_rows(up, -1) * cw[0:1] + up * cw[1:2] + _shift_rows(up, 1) * cw[2:3]
        return out[sl]

    for c in range(D_FF // chunk):
        cs = slice(c * chunk, (c + 1) * chunk)
        gate = conv3(_dot(xn, wg_ref[:, cs]), cg_ref[:, cs])
        val = conv3(_dot(xn, wv_ref[:, cs]), cv_ref[:, cs])
        inner = 0.7978845608028654 * (gate + 0.044715 * (gate * gate * gate))
        act_ref[:, cs] = (0.5 * gate * (1.0 + jnp.tanh(inner)) * val).astype(BF16)
    ff = _dot(act_ref[...], wd_ref[...])
    o_ref[...] = h1[sl] + _rms(ff, postw_ref[...])


def _mixffn(y_pool, y_attn, o_f, o_b, z, h, gdn_nw, bd256, w_out, post_mix_w,
            pre_w, w_up, conv_w, w_down, post_w, seq, tm, chunk, layer):
    n_tok = h.shape[0]
    const = lambda i: (0, 0)
    mat = lambda i: (layer, 0, 0)
    second = lambda i: (layer, 0, 1)
    resident = pl.Buffered(1)

    def halo(width, rows):
        per = tm // rows
        last = n_tok // rows - 1
        return [
            pl.BlockSpec((rows, width), lambda i: (jnp.maximum(i * per - 1, 0), 0)),
            pl.BlockSpec((tm, width), lambda i: (i, 0)),
            pl.BlockSpec((rows, width), lambda i: (jnp.minimum((i + 1) * per, last), 0)),
        ]

    bf16_rows = 2 * SUBLANES
    in_specs = (halo(POOL_WIDTH, bf16_rows) + halo(ATTN_WIDTH, bf16_rows) + halo(GDN_WIDTH, HALO) * 3
                + halo(D_MODEL, HALO) + [
        pl.BlockSpec((1, GDN_WIDTH), const),
        pl.BlockSpec((GDN_WIDTH, GDN_WIDTH), const),
        pl.BlockSpec((None, D_MODEL, D_MODEL), mat, pipeline_mode=resident),
        pl.BlockSpec((None, 1, D_MODEL), mat),
        pl.BlockSpec((None, 1, D_MODEL), mat),
        pl.BlockSpec((None, D_MODEL, D_FF), mat, pipeline_mode=resident),
        pl.BlockSpec((None, D_MODEL, D_FF), second, pipeline_mode=resident),
        pl.BlockSpec((None, FFN_CONV, D_FF), mat),
        pl.BlockSpec((None, FFN_CONV, D_FF), second),
        pl.BlockSpec((None, D_FF, D_MODEL), mat, pipeline_mode=resident),
        pl.BlockSpec((None, 1, D_MODEL), mat),
    ])
    return pl.pallas_call(
        functools.partial(_mixffn_kernel, tps=seq // tm, chunk=chunk),
        grid=(n_tok // tm,),
        in_specs=in_specs,
        out_specs=pl.BlockSpec((tm, D_MODEL), lambda i: (i, 0)),
        out_shape=jax.ShapeDtypeStruct((n_tok, D_MODEL), F32),
        scratch_shapes=[pltpu.VMEM((tm, D_FF), BF16)],
        compiler_params=_params(("parallel",)),
        name="mixffn",
    )(y_pool, y_pool, y_pool, y_attn, y_attn, y_attn, o_f, o_f, o_f, o_b, o_b, o_b, z, z, z, h, h, h,
      gdn_nw, bd256, w_out, post_mix_w, pre_w, w_up, w_up, conv_w, conv_w, w_down, post_w)


def _rope_tables(seq):
    t = np.arange(seq)
    pos = np.stack([t // GRID_W, t % GRID_W], axis=-1).astype(np.float32)
    axis_dim = HEAD_DIM // 2
    inv_freq = (ROPE_THETA ** (-np.arange(0, axis_dim, 2, dtype=np.float32) / axis_dim)).astype(np.float32)
    ang = pos[:, :, None] * inv_freq
    j = np.arange(LANES) % HEAD_DIM
    a = ang[:, j // 32, j % 16]
    sign = np.where((j % 32) < 16, -1.0, 1.0).astype(np.float32)
    return jnp.asarray(np.cos(a), F32), jnp.asarray(np.sin(a) * sign, F32)


def _head_blockdiag(n):
    i = np.arange(n) // HEAD_DIM
    return jnp.asarray(i[:, None] == i[None, :], BF16)


def _gate_expand():
    e = np.zeros((LANES, 4 * GDN_WIDTH), np.float32)
    for kind in range(2):
        for d in range(N_DIR):
            for hh in range(GDN_HEADS):
                src = kind * N_DIR * GDN_HEADS + d * GDN_HEADS + hh
                dst = (kind * N_DIR + d) * GDN_WIDTH + hh * GDN_DK
                e[src, dst:dst + GDN_DK] = 1.0
    return jnp.asarray(e, BF16)


def _pad_lanes(v, offset):
    out = jnp.zeros((1, LANES), F32)
    return out.at[0, offset:offset + v.shape[0]].set(v.astype(F32))


def kernel(x, pre_mix_norm, post_mix_norm, pre_ffn_norm, post_ffn_norm, w_in, pool_w, pool_scale, q_norm, k_norm, gdn_conv, gdn_a_log, gdn_dt_bias, gdn_norm, w_out, w_up, ffn_conv, w_down):
    batch, seq, _ = x.shape
    n_tok = batch * seq
    depth = w_in.shape[0]
    assert seq % 512 == 0 and seq % GRID_W == 0

    tm_proj = 512
    tm_gdn = 256
    tm_ffn = 512
    cos_t, sin_t = _rope_tables(seq)
    bd256 = _head_blockdiag(GDN_WIDTH)
    expand = _gate_expand()
    n_gate = N_DIR * GDN_HEADS

    w_cat = jnp.pad(w_in, ((0, 0), (0, 0), (0, N_CAT - D_IN))).astype(BF16)
    w_out_b = w_out.astype(BF16)
    w_up_b = w_up.astype(BF16)
    w_down_b = w_down.astype(BF16)

    h = x.reshape(n_tok, D_MODEL)
    for l in range(depth):
        qw = jnp.tile(q_norm[l], 2)[None, :]
        kw = jnp.tile(k_norm[l], 2)[None, :]
        bdw = jax.scipy.linalg.block_diag(*[pool_w[l, g] for g in range(POOL_GROUPS)]).astype(BF16)
        y_pool, qt, k4, vt, gqkv, z, gates = _inproj(
            h, pre_mix_norm[:, None, :], w_cat, cos_t, sin_t, qw, kw, bd256, bdw, pool_scale[l][None, :],
            seq, tm_proj, l)

        y_attn = _attention(qt, k4, vt, batch, seq, 256, 256, 16)

        neg_a = _pad_lanes(-jnp.exp(gdn_a_log[l].reshape(-1)), n_gate)
        dt_b = _pad_lanes(gdn_dt_bias[l].reshape(-1), n_gate)
        gq, gk, gv, be, gce, gcr, kgt = _gdn_prep(gqkv, gates, gdn_conv[l], neg_a, dt_b, bd256, expand, seq, tm_gdn)
        o_f, o_b = _gdn_scan(gq, gk, gv, be, gce, gcr, kgt, batch, seq, 8)

        h = _mixffn(y_pool, y_attn, o_f, o_b, z, h, jnp.tile(gdn_norm[l], GDN_HEADS)[None, :], bd256,
                    w_out_b, post_mix_norm[:, None, :], pre_ffn_norm[:, None, :], w_up_b, ffn_conv, w_down_b,
                    post_ffn_norm[:, None, :], seq, tm_ffn, 256, l)
    return h.reshape(batch, seq, D_MODEL)
```

```python
import functools
import math

import numpy as np
import jax
import jax.numpy as jnp
from jax import lax
from jax.experimental import pallas as pl
from jax.experimental.pallas import tpu as pltpu

F32 = jnp.float32
BF16 = jnp.bfloat16

D_MODEL = 1024
GRID_W = 64
EPS = 1e-6
POOL_WIDTH = 256
POOL_WINDOWS = (2, 4, 8, 16)
POOL_GROUPS = 4
POOL_GDIM = POOL_WIDTH // POOL_GROUPS
ATTN_HEADS = 8
ATTN_KV_HEADS = 2
HEAD_DIM = 64
ATTN_WIDTH = ATTN_HEADS * HEAD_DIM
KV_WIDTH = ATTN_KV_HEADS * HEAD_DIM
ROPE_THETA = 10000.0
GDN_HEADS = 4
GDN_DK = 64
GDN_WIDTH = GDN_HEADS * GDN_DK
GDN_CONV = 5
GDN_CHUNK = 64
N_DIR = 2
D_FF = 2816
FFN_CONV = 3
D_IN = POOL_WIDTH + ATTN_WIDTH + 2 * KV_WIDTH + 4 * GDN_WIDTH + 2 * N_DIR * GDN_HEADS

LANES = 128
SUBLANES = 8
HALO = SUBLANES
VMEM_LIMIT = 56 * 1024 * 1024

N_CAT = 2176
COL_POOL = 0
COL_QA = COL_POOL + POOL_WIDTH
COL_KA = COL_QA + ATTN_WIDTH
COL_VA = COL_KA + KV_WIDTH
COL_GQKV = COL_VA + KV_WIDTH
COL_Z = COL_GQKV + 3 * GDN_WIDTH
COL_GATES = COL_Z + GDN_WIDTH

Q_SCALE = HEAD_DIM ** -0.5 * math.log2(math.e)
V_ROWS = HEAD_DIM + 16


def _dot(a, b):
    return jnp.dot(a, b, preferred_element_type=F32)


def _dot_nt(a, b):
    return lax.dot_general(a, b, (((1,), (1,)), ((), ())), preferred_element_type=F32)


def _split2(x):
    hi = x.astype(BF16)
    lo = (x - hi.astype(F32)).astype(BF16)
    return hi, lo


def _sel_dot(m01, x):
    hi, lo = _split2(x)
    return _dot(m01, hi) + _dot(m01, lo)


def _dot_sel(x, m01):
    hi, lo = _split2(x)
    return _dot(hi, m01) + _dot(lo, m01)


def _group_sum(x, bd):
    return _dot_sel(x, bd)


def _rms(x, w):
    ms = jnp.mean(x * x, axis=-1, keepdims=True)
    return x * lax.rsqrt(ms + EPS) * w


def _shift_rows(x, d):
    n = x.shape[0]
    return pltpu.roll(x, (-d) % n, 0)


def _params(sem):
    return pltpu.CompilerParams(dimension_semantics=sem, vmem_limit_bytes=VMEM_LIMIT)


def _halo_specs(tm, width, n_rows):
    per = tm // HALO
    last = n_rows // HALO - 1
    return [
        pl.BlockSpec((HALO, width), lambda i: (jnp.maximum(i * per - 1, 0), 0)),
        pl.BlockSpec((tm, width), lambda i: (i, 0)),
        pl.BlockSpec((HALO, width), lambda i: (jnp.minimum((i + 1) * per, last), 0)),
    ]


def _seq_edges(tiles_per_seq):
    t = pl.program_id(0) % tiles_per_seq
    return t, t == 0, t == tiles_per_seq - 1


def _inproj_kernel(xp_ref, x_ref, xnx_ref, nw_ref, w_ref, cos_ref, sin_ref, qw_ref, kw_ref, bd_ref,
                   bdw_ref, psc_ref, yp_ref, q_ref, k_ref, vt_ref, g_ref, z_ref, gt_ref, *, tps, seq):
    tm = x_ref.shape[0]
    xn = _rms(x_ref[...], nw_ref[...]).astype(BF16)

    cos = cos_ref[...]
    sin = sin_ref[...]
    lane = lax.broadcasted_iota(jnp.int32, (tm, LANES), 1)
    first_half = (lane & 16) == 0
    lo64 = lane < HEAD_DIM
    bd = bd_ref[...]

    def head_norm_rope(x, ss, w):
        y = x * lax.rsqrt(ss * (1.0 / HEAD_DIM) + EPS) * w
        partner = jnp.where(first_half, pltpu.roll(y, LANES - 16, 1), pltpu.roll(y, 16, 1))
        return y * cos + partner * sin

    n_slab = ATTN_WIDTH // LANES
    qa = _dot(xn, w_ref[:, COL_QA:COL_KA])
    ka = _dot(xn, w_ref[:, COL_KA:COL_VA])
    ss_q = [_group_sum(qa[:, p * 2 * LANES:(p + 1) * 2 * LANES] ** 2, bd) for p in range(n_slab // 2)]
    ss_q = [ss_q[s // 2][:, (s % 2) * LANES:(s % 2 + 1) * LANES] for s in range(n_slab)]
    qa = [qa[:, s * LANES:(s + 1) * LANES] for s in range(n_slab)]
    ss_k = _group_sum(ka * ka, bd[0:LANES, 0:LANES])

    t, first, last = _seq_edges(tps)
    w_pool = w_ref[:, COL_POOL:COL_QA]
    u = _dot(xn, w_pool)
    xh = _rms(jnp.concatenate([xp_ref[...], xnx_ref[...]], axis=0), nw_ref[...]).astype(BF16)
    uh = _dot(xh, w_pool)
    ext = jnp.concatenate([jnp.where(first, 0.0, uh[0:HALO]), u, jnp.where(last, 0.0, uh[HALO:2 * HALO])], axis=0)
    w2 = ext + _shift_rows(ext, -1)
    w4 = _shift_rows(w2, -1) + _shift_rows(w2, 1)
    w8 = _shift_rows(w4, -2) + _shift_rows(w4, 2)
    w16 = _shift_rows(w8, -4) + _shift_rows(w8, 4)
    sl = slice(HALO, HALO + tm)
    grp = lax.broadcasted_iota(jnp.int32, (tm, POOL_WIDTH), 1) // POOL_GDIM
    seg = jnp.where(grp == 0, w2[sl], jnp.where(grp == 1, w4[sl], jnp.where(grp == 2, w8[sl], w16[sl])))
    half = jnp.where(grp == 0, 1, jnp.where(grp == 1, 2, jnp.where(grp == 2, 4, 8)))
    pos = t * tm + lax.broadcasted_iota(jnp.int32, (tm, POOL_WIDTH), 0)
    cnt = (jnp.minimum(pos + half, seq) - jnp.maximum(pos - half, 0)).astype(F32)
    pooled = (seg / cnt - u).astype(BF16)

    qw = qw_ref[...] * Q_SCALE
    q_rot = jnp.concatenate([head_norm_rope(qa[s], ss_q[s], qw) for s in range(n_slab)], axis=1)
    q_ref[...] = q_rot.T.astype(BF16)

    kr = head_norm_rope(ka, ss_k, kw_ref[...])
    ks = pltpu.roll(kr, HEAD_DIM, 1)
    k_ref[:, 0:128] = jnp.where(lo64, kr, 0.0).astype(BF16)
    k_ref[:, 128:256] = jnp.where(lo64, 0.0, ks).astype(BF16)
    k_ref[:, 256:384] = jnp.where(lo64, ks, 0.0).astype(BF16)
    k_ref[:, 384:512] = jnp.where(lo64, 0.0, kr).astype(BF16)

    vat = _dot(xn, w_ref[:, COL_VA:COL_GQKV]).T
    pad_row = lax.broadcasted_iota(jnp.int32, (V_ROWS - HEAD_DIM, tm), 0)
    ones_rows = jnp.where(pad_row == 0, 1.0, 0.0)
    vt_ref[...] = jnp.concatenate(
        [vat[0:HEAD_DIM], ones_rows, vat[HEAD_DIM:2 * HEAD_DIM], ones_rows], axis=0).astype(BF16)
    g_ref[...] = _dot(xn, w_ref[:, COL_GQKV:COL_Z])
    z_ref[...] = _dot(xn, w_ref[:, COL_Z:COL_GATES])
    gt_ref[...] = _dot(xn, w_ref[:, COL_GATES:N_CAT])
    yp_ref[...] = (_dot(pooled, bdw_ref[...]) * psc_ref[...]).astype(BF16)


def _inproj(x2d, norm_w, w_cat, cos_t, sin_t, qw, kw, bd256, bdw, pool_scale, seq, tm, layer):
    n_tok = x2d.shape[0]
    tps = seq // tm
    const = lambda i: (0, 0)
    row = lambda i: (i, 0)
    pos = lambda i: (i % tps, 0)
    return pl.pallas_call(
        functools.partial(_inproj_kernel, tps=tps, seq=seq),
        grid=(n_tok // tm,),
        in_specs=_halo_specs(tm, D_MODEL, n_tok) + [
            pl.BlockSpec((None, 1, D_MODEL), lambda i: (layer, 0, 0)),
            pl.BlockSpec((None, D_MODEL, N_CAT), lambda i: (layer, 0, 0)),
            pl.BlockSpec((tm, LANES), pos),
            pl.BlockSpec((tm, LANES), pos),
            pl.BlockSpec((1, LANES), const),
            pl.BlockSpec((1, LANES), const),
            pl.BlockSpec((2 * LANES, 2 * LANES), const),
            pl.BlockSpec((POOL_WIDTH, POOL_WIDTH), const),
            pl.BlockSpec((1, POOL_WIDTH), const),
        ],
        out_specs=[
            pl.BlockSpec((tm, POOL_WIDTH), row),
            pl.BlockSpec((ATTN_WIDTH, tm), lambda i: (0, i)),
            pl.BlockSpec((tm, 4 * LANES), row),
            pl.BlockSpec((ATTN_KV_HEADS * V_ROWS, tm), lambda i: (0, i)),
            pl.BlockSpec((tm, 3 * GDN_WIDTH), row),
            pl.BlockSpec((tm, GDN_WIDTH), row),
            pl.BlockSpec((tm, LANES), row),
        ],
        out_shape=[
            jax.ShapeDtypeStruct((n_tok, POOL_WIDTH), BF16),
            jax.ShapeDtypeStruct((ATTN_WIDTH, n_tok), BF16),
            jax.ShapeDtypeStruct((n_tok, 4 * LANES), BF16),
            jax.ShapeDtypeStruct((ATTN_KV_HEADS * V_ROWS, n_tok), BF16),
            jax.ShapeDtypeStruct((n_tok, 3 * GDN_WIDTH), F32),
            jax.ShapeDtypeStruct((n_tok, GDN_WIDTH), F32),
            jax.ShapeDtypeStruct((n_tok, LANES), F32),
        ],
        compiler_params=_params(("parallel",)),
        name="inproj",
    )(x2d, x2d, x2d, norm_w, w_cat, cos_t, sin_t, qw, kw, bd256, bdw, pool_scale)


def _attn_kernel(q_ref, qn_ref, k_ref, vt_ref, o_ref, acc_ref, m_ref, s_ref, bm_ref, qq_ref, *, tk, per_iter):
    seq = k_ref.shape[0]
    group = ATTN_HEADS // ATTN_KV_HEADS
    n_chunks = seq // tk
    n_iter = n_chunks // per_iter
    acc_ref[...] = jnp.zeros_like(acc_ref)
    m_ref[...] = jnp.full_like(m_ref, -1e30)
    qq_ref[0] = q_ref[...]
    qq_ref[1] = qn_ref[...]

    def scores(c, buf, slab, tile=0):
        r0 = pl.multiple_of(c * tk, tk)
        kk = jnp.concatenate([k_ref[pl.ds(r0, tk), 0:LANES], k_ref[pl.ds(r0, tk), LANES:2 * LANES]], axis=0)
        s2 = _dot(kk, qq_ref[tile, slab * LANES:(slab + 1) * LANES, :])
        for half in range(2):
            j = 2 * slab + half
            s = s2[half * tk:(half + 1) * tk]
            s_ref[buf, j] = s
            bm_ref[buf, j] = jnp.max(s, axis=0, keepdims=True)

    def softmax_pv(c, buf, slab):
        r0 = pl.multiple_of(c * tk, tk)
        vt = vt_ref[:, pl.ds(r0, tk)]
        for j in (2 * slab, 2 * slab + 1):
            m = m_ref[j]
            m_new = jnp.maximum(m, bm_ref[buf, j])
            alpha = jnp.exp2(m - m_new)
            p = jnp.exp2(s_ref[buf, j] - m_new)
            m_ref[j] = m_new
            acc_ref[j] = alpha * acc_ref[j] + _dot(vt, p.astype(BF16))

    slabs = range(group // 2)

    @pl.when(pl.program_id(2) == 0)
    def _():
        for slab in slabs:
            scores(0, 0, slab)

    def body(i, carry):
        c0 = per_iter * i
        wrap = (i == n_iter - 1).astype(jnp.int32)
        for u in range(per_iter):
            for slab in slabs:
                if u == per_iter - 1:
                    scores((1 - wrap) * (c0 + per_iter), 0, slab, tile=wrap)
                else:
                    scores(c0 + u + 1, (u + 1) % 2, slab)
            for slab in slabs:
                softmax_pv(c0 + u, u % 2, slab)
        return carry

    lax.fori_loop(0, n_iter, body, 0)
    out = jnp.concatenate([acc_ref[j, 0:HEAD_DIM] / acc_ref[j, HEAD_DIM:HEAD_DIM + 1] for j in range(group)], axis=0)
    o_ref[...] = out.T.astype(BF16)


def _attention(qt, k4, vt, batch, seq, tq, tk, per_iter):
    n_tok = k4.shape[0]
    nq = seq // tq
    group = ATTN_HEADS // ATTN_KV_HEADS
    gw = group * HEAD_DIM
    per_iter = min(per_iter, seq // tk)
    assert per_iter % 2 == 0 and (seq // tk) % per_iter == 0
    return pl.pallas_call(
        functools.partial(_attn_kernel, tk=tk, per_iter=per_iter),
        grid=(batch, ATTN_KV_HEADS, nq),
        in_specs=[
            pl.BlockSpec((gw, tq), lambda b, g, i: (g, b * nq + i)),
            pl.BlockSpec((gw, tq), lambda b, g, i: (g, b * nq + jnp.minimum(i + 1, nq - 1))),
            pl.BlockSpec((seq, 2 * LANES), lambda b, g, i: (b, g)),
            pl.BlockSpec((V_ROWS, seq), lambda b, g, i: (g, b)),
        ],
        out_specs=pl.BlockSpec((tq, gw), lambda b, g, i: (b * nq + i, g)),
        out_shape=jax.ShapeDtypeStruct((n_tok, ATTN_WIDTH), BF16),
        scratch_shapes=[pltpu.VMEM((group, V_ROWS, tq), F32),
                        pltpu.VMEM((group, 1, tq), F32),
                        pltpu.VMEM((2, group, tk, tq), F32),
                        pltpu.VMEM((2, group, 1, tq), F32),
                        pltpu.VMEM((2, gw, tq), BF16)],
        compiler_params=_params(("parallel", "parallel", "arbitrary")),
        name="attention",
    )(qt, qt, k4, vt)


def _gdn_prep_kernel(prev_ref, x_ref, next_ref, gt_ref, cw_ref, na_ref, dtb_ref, bd_ref, ex_ref,
                     q_ref, k_ref, v_ref, be_ref, gce_ref, gcr_ref, kgt_ref, *, tps):
    tm = x_ref.shape[0]
    _, first, last = _seq_edges(tps)
    prev = jnp.where(first, 0.0, prev_ref[...])
    nxt = jnp.where(last, 0.0, next_ref[...])
    ext = jnp.concatenate([prev, x_ref[...], nxt], axis=0)
    cw = cw_ref[...]
    conv = ext * cw[2:3]
    for kk in (0, 1, 3, 4):
        conv = conv + _shift_rows(ext, kk - GDN_CONV // 2) * cw[kk:kk + 1]
    conv = conv[HALO:HALO + tm]
    act = conv / (1.0 + jnp.exp(-conv))
    bd = bd_ref[...]

    def l2n(x):
        return x * lax.rsqrt(_group_sum(x * x, bd) + EPS)

    q = l2n(act[:, 0:GDN_WIDTH]) * (GDN_DK ** -0.5)
    k = l2n(act[:, GDN_WIDTH:2 * GDN_WIDTH])
    q_ref[...] = q
    k_ref[...] = k
    v_ref[...] = act[:, 2 * GDN_WIDTH:3 * GDN_WIDTH]

    gt = gt_ref[...]
    lane = lax.broadcasted_iota(jnp.int32, (tm, LANES), 1)
    beta = 1.0 / (1.0 + jnp.exp(-gt))
    xs = gt + dtb_ref[...]
    softplus = jnp.maximum(xs, 0.0) + jnp.log1p(jnp.exp(-jnp.abs(xs)))
    g = na_ref[...] * softplus

    r = lax.broadcasted_iota(jnp.int32, (tm, tm), 0)
    c = lax.broadcasted_iota(jnp.int32, (tm, tm), 1)
    same = (r // GDN_CHUNK) == (c // GDN_CHUNK)
    ones_bd = jnp.where(same, 1.0, 0.0).astype(BF16)
    tri_f = jnp.where(same & (c <= r), 1.0, 0.0).astype(BF16)
    tri_b = jnp.where(same & (c >= r), 1.0, 0.0).astype(BF16)
    gc = jnp.where(lane < 12, _sel_dot(tri_f, g), _sel_dot(tri_b, g))
    comp = jnp.where(lane < 8, beta, gc)
    exp_all = _dot_sel(comp, ex_ref[...])
    tot = _sel_dot(ones_bd, g)
    tot_e = _dot_sel(tot, ex_ref[:, 2 * GDN_WIDTH:4 * GDN_WIDTH])

    row = lax.broadcasted_iota(jnp.int32, (tm, GDN_WIDTH), 0)
    col = lax.broadcasted_iota(jnp.int32, (tm, GDN_WIDTH), 1)
    diag = (row % GDN_CHUNK) == (col % GDN_DK)
    for d in range(N_DIR):
        be_ref[d] = exp_all[:, d * GDN_WIDTH:(d + 1) * GDN_WIDTH]
        gce = exp_all[:, (2 + d) * GDN_WIDTH:(3 + d) * GDN_WIDTH]
        gce_ref[d] = gce
        gcr_ref[d] = _sel_dot(ones_bd, jnp.where(diag, gce, 0.0))
        kg = k * jnp.exp(tot_e[:, d * GDN_WIDTH:(d + 1) * GDN_WIDTH] - gce)
        lo_lane = lax.broadcasted_iota(jnp.int32, (GDN_DK, LANES), 1) < GDN_DK
        for pr in range(tm // LANES):
            xt = kg[pr * LANES:(pr + 1) * LANES, :].T
            xr = pltpu.roll(xt, GDN_CHUNK, 1)
            for cl in range(2):
                even, odd = (xt, xr) if cl == 0 else (xr, xt)
                tiles = [jnp.where(lo_lane, even[(2 * p) * GDN_DK:(2 * p + 1) * GDN_DK],
                                   odd[(2 * p + 1) * GDN_DK:(2 * p + 2) * GDN_DK]) for p in range(GDN_HEADS // 2)]
                r0 = (2 * pr + cl) * GDN_CHUNK
                kgt_ref[d, r0:r0 + GDN_CHUNK, :] = jnp.concatenate(tiles, axis=1).astype(BF16)


def _gdn_prep(gqkv, gates, conv_w, neg_a, dt_b, bd256, expand, seq, tm):
    n_tok = gqkv.shape[0]
    width = 3 * GDN_WIDTH
    const = lambda i: (0, 0)
    row = lambda i: (i, 0)
    drow = lambda i: (0, i, 0)
    tok = jax.ShapeDtypeStruct((n_tok, GDN_WIDTH), F32)
    dtok = jax.ShapeDtypeStruct((N_DIR, n_tok, GDN_WIDTH), F32)
    return pl.pallas_call(
        functools.partial(_gdn_prep_kernel, tps=seq // tm),
        grid=(n_tok // tm,),
        in_specs=_halo_specs(tm, width, n_tok) + [
            pl.BlockSpec((tm, LANES), row),
            pl.BlockSpec((GDN_CONV, width), const),
            pl.BlockSpec((1, LANES), const),
            pl.BlockSpec((1, LANES), const),
            pl.BlockSpec((GDN_WIDTH, GDN_WIDTH), const),
            pl.BlockSpec((LANES, 4 * GDN_WIDTH), const),
        ],
        out_specs=[
            pl.BlockSpec((tm, GDN_WIDTH), row),
            pl.BlockSpec((tm, GDN_WIDTH), row),
            pl.BlockSpec((tm, GDN_WIDTH), row),
            pl.BlockSpec((N_DIR, tm, GDN_WIDTH), drow),
            pl.BlockSpec((N_DIR, tm, GDN_WIDTH), drow),
            pl.BlockSpec((N_DIR, tm, GDN_WIDTH), drow),
            pl.BlockSpec((N_DIR, tm, GDN_WIDTH), drow),
        ],
        out_shape=[tok, tok, tok, dtok, dtok, dtok,
                   jax.ShapeDtypeStruct((N_DIR, n_tok, GDN_WIDTH), BF16)],
        compiler_params=_params(("parallel",)),
        name="gdn_prep",
    )(gqkv, gqkv, gqkv, gates, conv_w, neg_a, dt_b, bd256, expand)


def _gdn_scan_kernel(qf_ref, kf_ref, vf_ref, bef_ref, gcef_ref, gcrf_ref,
                     qb_ref, kb_ref, vb_ref, beb_ref, gceb_ref, gcrb_ref, kgpf_ref, kgpb_ref,
                     of_ref, ob_ref, state_ref, stage_ref, gl_ref, *, chunks, wave):
    cs = GDN_CHUNK
    w = GDN_WIDTH
    par = pl.program_id(1) % 2
    prv = 1 - par

    @pl.when(pl.program_id(1) == 0)
    def _():
        state_ref[...] = jnp.zeros_like(state_ref)
        stage_ref[...] = jnp.zeros_like(stage_ref)
        gl_ref[...] = jnp.zeros_like(gl_ref)

    lane_h = lax.broadcasted_iota(jnp.int32, (cs, w), 1) // GDN_DK
    heads = [lane_h == h for h in range(GDN_HEADS)]
    ti = lax.broadcasted_iota(jnp.int32, (cs, w), 0)
    tj = lax.broadcasted_iota(jnp.int32, (cs, w), 1) % cs
    eye = (ti == tj).astype(F32)
    in_refs = ((qf_ref, kf_ref, vf_ref, bef_ref, gcef_ref, gcrf_ref),
               (qb_ref, kb_ref, vb_ref, beb_ref, gceb_ref, gcrb_ref))
    kgp_refs = (kgpf_ref, kgpb_ref)
    o_refs = (of_ref, ob_ref)
    U, W, QG, QKM = range(4)

    def blockdiag(y):
        return jnp.concatenate([jnp.where(hm, y, 0.0) for hm in heads], axis=0).astype(BF16)

    def rows_of(ci):
        return slice(ci * cs, (ci + 1) * cs)

    loc = {}

    def local_scores(d, ci):
        q_ref, k_ref, v_ref, be_ref, gce_ref, gcr_ref = in_refs[d]
        rs = rows_of(ci)
        q = q_ref[rs, :]
        k = k_ref[rs, :]
        be = be_ref[0, rs, :]
        kb = k * be
        kq = _dot_nt(jnp.concatenate([kb, q], axis=0).astype(BF16), blockdiag(k))
        loc[d, ci] = dict(kq=kq, kb=kb)

    def local_decay(d, ci):
        st = loc[d, ci]
        _, _, _, _, gce_ref, gcr_ref = in_refs[d]
        rs = rows_of(ci)
        incl = (ti >= tj) if d == 0 else (ti <= tj)
        strict = (ti > tj) if d == 0 else (ti < tj)
        diff = gce_ref[0, rs, :] - gcr_ref[0, rs, :]
        decay = jnp.where(incl, jnp.exp(jnp.where(incl, diff, 0.0)), 0.0)
        kq = st.pop("kq")
        a = jnp.where(strict, kq[0:cs] * decay, 0.0)
        stage_ref[par, d, QKM, rs, :] = kq[cs:2 * cs] * decay
        st["pw"] = _dot(a.astype(BF16), blockdiag(a))
        st["t"] = eye - a

    def local_invert(d, ci, last):
        st = loc[d, ci]
        bdp = blockdiag(st["pw"])
        if last:
            st["t"] = st["t"] + _dot(st["t"].astype(BF16), bdp)
        else:
            both = _dot(jnp.concatenate([st["pw"], st["t"]], axis=0).astype(BF16), bdp)
            st["pw"] = both[0:cs]
            st["t"] = st["t"] + both[cs:2 * cs]

    def local_uw(d, ci):
        st = loc.pop((d, ci))
        q_ref, _, v_ref, be_ref, gce_ref, _ = in_refs[d]
        rs = rows_of(ci)
        eg = jnp.exp(gce_ref[0, rs, :])
        vb = v_ref[rs, :] * be_ref[0, rs, :]
        rhs = jnp.concatenate([blockdiag(vb), blockdiag(st["kb"] * eg)], axis=1)
        uw = _dot(st["t"].astype(BF16), rhs)
        stage_ref[par, d, U, rs, :] = uw[:, 0:w]
        stage_ref[par, d, W, rs, :] = uw[:, w:2 * w]
        stage_ref[par, d, QG, rs, :] = q_ref[rs, :] * eg
        edge = (ci + 1) * cs - 1 if d == 0 else ci * cs
        gl_ref[par, d, ci] = jnp.exp(gce_ref[0, edge:edge + 1, :])

    scan = {}

    def scan_read(d, ci):
        rs = rows_of(ci)
        lhs = jnp.concatenate([stage_ref[prv, d, W, rs, :], stage_ref[prv, d, QG, rs, :]], axis=0)
        scan[d] = _dot(lhs.astype(BF16), blockdiag(state_ref[d]))

    def scan_out(d, ci):
        rs = rows_of(ci)
        ws = scan[d]
        vn = stage_ref[prv, d, U, rs, :] - ws[0:cs]
        lhs = jnp.concatenate([stage_ref[prv, d, QKM, rs, :].astype(BF16), kgp_refs[d][0, rs, :]], axis=0)
        res = _dot(lhs, blockdiag(vn))
        o_refs[d][rs, :] = ws[cs:2 * cs] + res[0:cs]
        state_ref[d] = state_ref[d] * gl_ref[prv, d, ci] + res[cs:2 * cs]

    order = [(d, ci if d == 0 else chunks - 1 - ci) for ci in range(chunks) for d in range(N_DIR)]
    local_ops = []
    for w0 in range(0, len(order), wave):
        wv = order[w0:w0 + wave]
        local_ops += [functools.partial(local_scores, d, ci) for d, ci in wv]
        local_ops += [functools.partial(local_decay, d, ci) for d, ci in wv]
        for step in range(5):
            local_ops += [functools.partial(local_invert, d, ci, step == 4) for d, ci in wv]
        local_ops += [functools.partial(local_uw, d, ci) for d, ci in wv]
    scan_ops = []
    for ci in range(chunks):
        for fn in (scan_read, scan_out):
            scan_ops.append([functools.partial(fn, d, ci if d == 0 else chunks - 1 - ci) for d in range(N_DIR)])
    every = len(local_ops) // len(scan_ops)
    for idx, op in enumerate(local_ops):
        op()
        if (idx + 1) % every == 0 and scan_ops:
            for s_op in scan_ops.pop(0):
                s_op()
    for level in scan_ops:
        for s_op in level:
            s_op()


def _gdn_scan(q, k, v, be, gce, gcr, kgp, batch, seq, chunks):
    n_tok = q.shape[0]
    rows = chunks * GDN_CHUNK
    ng = seq // rows
    w = GDN_WIDTH

    def group(d, lag):
        def f(b, n):
            g = jnp.clip(n - lag, 0, ng - 1)
            return b * ng + (ng - 1 - g if d == 1 else g)
        return f

    def tok(d, lag):
        f = group(d, lag)
        return lambda b, n: (f(b, n), 0)

    def dtok(d, lag):
        f = group(d, lag)
        return lambda b, n: (d, f(b, n), 0)

    def local_specs(d):
        return [pl.BlockSpec((rows, w), tok(d, 0))] * 3 + [pl.BlockSpec((1, rows, w), dtok(d, 0))] * 3

    out = jax.ShapeDtypeStruct((n_tok, w), F32)
    return pl.pallas_call(
        functools.partial(_gdn_scan_kernel, chunks=chunks, wave=2 * chunks),
        grid=(batch, ng + 1),
        in_specs=local_specs(0) + local_specs(1) + [pl.BlockSpec((1, rows, w), dtok(d, 1)) for d in range(N_DIR)],
        out_specs=[pl.BlockSpec((rows, w), tok(d, 1)) for d in range(N_DIR)],
        out_shape=[out, out],
        scratch_shapes=[pltpu.VMEM((N_DIR, GDN_DK, w), F32),
                        pltpu.VMEM((2, N_DIR, 4, rows, w), F32),
                        pltpu.VMEM((2, N_DIR, chunks, 1, w), F32)],
        compiler_params=_params(("parallel", "arbitrary")),
        name="gdn_scan",
    )(q, k, v, be, gce, gcr, q, k, v, be, gce, gcr, kgp, kgp)


def _mixffn_kernel(yp_p, yp_m, yp_n, ya_p, ya_m, ya_n, of_p, of_m, of_n, ob_p, ob_m, ob_n,
                   z_p, z_m, z_n, h_p, h_m, h_n, gw_ref, bd_ref, wo_ref, postmix_ref,
                   prew_ref, wg_ref, wv_ref, cg_ref, cv_ref, wd_ref, postw_ref, o_ref, act_ref, *, tps, chunk):
    tm = h_m.shape[0]
    _, first, last = _seq_edges(tps)

    def ext(p_ref, m_ref, n_ref):
        p = p_ref[...].astype(F32)
        n = n_ref[...].astype(F32)
        return jnp.concatenate([p[p.shape[0] - HALO:], m_ref[...].astype(F32), n[0:HALO]], axis=0)

    a0 = POOL_WIDTH
    a1 = POOL_WIDTH + ATTN_WIDTH
    mix = _dot(ext(yp_p, yp_m, yp_n).astype(BF16), wo_ref[0:a0, :])
    mix = mix + _dot(ext(ya_p, ya_m, ya_n).astype(BF16), wo_ref[a0:a1, :])
    o = ext(of_p, of_m, of_n) + ext(ob_p, ob_m, ob_n)
    ms = _group_sum(o * o, bd_ref[...]) * (1.0 / GDN_DK)
    z = ext(z_p, z_m, z_n)
    yg = o * lax.rsqrt(ms + EPS) * gw_ref[...] * (z / (1.0 + jnp.exp(-z)))
    mix = mix + _dot(yg.astype(BF16), wo_ref[a1:D_MODEL, :])
    h1 = ext(h_p, h_m, h_n) + _rms(mix, postmix_ref[...])

    row = lax.broadcasted_iota(jnp.int32, (tm + 2 * HALO, 1), 0)
    outside = (first & (row < HALO)) | (last & (row >= HALO + tm))
    xn = jnp.where(outside, 0.0, _rms(h1, prew_ref[...])).astype(BF16)
    sl = slice(HALO, HALO + tm)

    def conv3(up, cw):
        out = _shift_rows(up, -1) * cw[0:1] + up * cw[1:2] + _shift_rows(up, 1) * cw[2:3]
        return out[sl]

    for c in range(D_FF // chunk):
        cs = slice(c * chunk, (c + 1) * chunk)
        gate = conv3(_dot(xn, wg_ref[:, cs]), cg_ref[:, cs])
        val = conv3(_dot(xn, wv_ref[:, cs]), cv_ref[:, cs])
        inner = 0.7978845608028654 * (gate + 0.044715 * (gate * gate * gate))
        act_ref[:, cs] = (0.5 * gate * (1.0 + jnp.tanh(inner)) * val).astype(BF16)
    ff = _dot(act_ref[...], wd_ref[...])
    o_ref[...] = h1[sl] + _rms(ff, postw_ref[...])


def _mixffn(y_pool, y_attn, o_f, o_b, z, h, gdn_nw, bd256, w_out, post_mix_w,
            pre_w, w_up, conv_w, w_down, post_w, seq, tm, chunk, layer):
    n_tok = h.shape[0]
    const = lambda i: (0, 0)
    mat = lambda i: (layer, 0, 0)
    second = lambda i: (layer, 0, 1)
    resident = pl.Buffered(1)

    def halo(width, rows):
        per = tm // rows
        last = n_tok // rows - 1
        return [
            pl.BlockSpec((rows, width), lambda i: (jnp.maximum(i * per - 1, 0), 0)),
            pl.BlockSpec((tm, width), lambda i: (i, 0)),
            pl.BlockSpec((rows, width), lambda i: (jnp.minimum((i + 1) * per, last), 0)),
        ]

    bf16_rows = 2 * SUBLANES
    in_specs = (halo(POOL_WIDTH, bf16_rows) + halo(ATTN_WIDTH, bf16_rows) + halo(GDN_WIDTH, HALO) * 3
                + halo(D_MODEL, HALO) + [
        pl.BlockSpec((1, GDN_WIDTH), const),
        pl.BlockSpec((GDN_WIDTH, GDN_WIDTH), const),
        pl.BlockSpec((None, D_MODEL, D_MODEL), mat, pipeline_mode=resident),
        pl.BlockSpec((None, 1, D_MODEL), mat),
        pl.BlockSpec((None, 1, D_MODEL), mat),
        pl.BlockSpec((None, D_MODEL, D_FF), mat, pipeline_mode=resident),
        pl.BlockSpec((None, D_MODEL, D_FF), second, pipeline_mode=resident),
        pl.BlockSpec((None, FFN_CONV, D_FF), mat),
        pl.BlockSpec((None, FFN_CONV, D_FF), second),
        pl.BlockSpec((None, D_FF, D_MODEL), mat, pipeline_mode=resident),
        pl.BlockSpec((None, 1, D_MODEL), mat),
    ])
    return pl.pallas_call(
        functools.partial(_mixffn_kernel, tps=seq // tm, chunk=chunk),
        grid=(n_tok // tm,),
        in_specs=in_specs,
        out_specs=pl.BlockSpec((tm, D_MODEL), lambda i: (i, 0)),
        out_shape=jax.ShapeDtypeStruct((n_tok, D_MODEL), F32),
        scratch_shapes=[pltpu.VMEM((tm, D_FF), BF16)],
        compiler_params=_params(("parallel",)),
        name="mixffn",
    )(y_pool, y_pool, y_pool, y_attn, y_attn, y_attn, o_f, o_f, o_f, o_b, o_b, o_b, z, z, z, h, h, h,
      gdn_nw, bd256, w_out, post_mix_w, pre_w, w_up, w_up, conv_w, conv_w, w_down, post_w)


def _rope_tables(seq):
    t = np.arange(seq)
    pos = np.stack([t // GRID_W, t % GRID_W], axis=-1).astype(np.float32)
    axis_dim = HEAD_DIM // 2
    inv_freq = (ROPE_THETA ** (-np.arange(0, axis_dim, 2, dtype=np.float32) / axis_dim)).astype(np.float32)
    ang = pos[:, :, None] * inv_freq
    j = np.arange(LANES) % HEAD_DIM
    a = ang[:, j // 32, j % 16]
    sign = np.where((j % 32) < 16, -1.0, 1.0).astype(np.float32)
    return jnp.asarray(np.cos(a), F32), jnp.asarray(np.sin(a) * sign, F32)


def _head_blockdiag(n):
    i = np.arange(n) // HEAD_DIM
    return jnp.asarray(i[:, None] == i[None, :], BF16)


def _gate_expand():
    e = np.zeros((LANES, 4 * GDN_WIDTH), np.float32)
    for kind in range(2):
        for d in range(N_DIR):
            for hh in range(GDN_HEADS):
                src = kind * N_DIR * GDN_HEADS + d * GDN_HEADS + hh
                dst = (kind * N_DIR + d) * GDN_WIDTH + hh * GDN_DK
                e[src, dst:dst + GDN_DK] = 1.0
    return jnp.asarray(e, BF16)


def _pad_lanes(v, offset):
    out = jnp.zeros((1, LANES), F32)
    return out.at[0, offset:offset + v.shape[0]].set(v.astype(F32))


def kernel(x, pre_mix_norm, post_mix_norm, pre_ffn_norm, post_ffn_norm, w_in, pool_w, pool_scale, q_norm, k_norm, gdn_conv, gdn_a_log, gdn_dt_bias, gdn_norm, w_out, w_up, ffn_conv, w_down):
    batch, seq, _ = x.shape
    n_tok = batch * seq
    depth = w_in.shape[0]
    assert seq % 512 == 0 and seq % GRID_W == 0

    tm_proj = 512
    tm_gdn = 256
    tm_ffn = 512
    cos_t, sin_t = _rope_tables(seq)
    bd256 = _head_blockdiag(GDN_WIDTH)
    expand = _gate_expand()
    n_gate = N_DIR * GDN_HEADS

    w_cat = jnp.pad(w_in, ((0, 0), (0, 0), (0, N_CAT - D_IN))).astype(BF16)
    w_out_b = w_out.astype(BF16)
    w_up_b = w_up.astype(BF16)
    w_down_b = w_down.astype(BF16)

    h = x.reshape(n_tok, D_MODEL)
    for l in range(depth):
        qw = jnp.tile(q_norm[l], 2)[None, :]
        kw = jnp.tile(k_norm[l], 2)[None, :]
        bdw = jax.scipy.linalg.block_diag(*[pool_w[l, g] for g in range(POOL_GROUPS)]).astype(BF16)
        y_pool, qt, k4, vt, gqkv, z, gates = _inproj(
            h, pre_mix_norm[:, None, :], w_cat, cos_t, sin_t, qw, kw, bd256, bdw, pool_scale[l][None, :],
            seq, tm_proj, l)

        y_attn = _attention(qt, k4, vt, batch, seq, 256, 256, 16)

        neg_a = _pad_lanes(-jnp.exp(gdn_a_log[l].reshape(-1)), n_gate)
        dt_b = _pad_lanes(gdn_dt_bias[l].reshape(-1), n_gate)
        gq, gk, gv, be, gce, gcr, kgt = _gdn_prep(gqkv, gates, gdn_conv[l], neg_a, dt_b, bd256, expand, seq, tm_gdn)
        o_f, o_b = _gdn_scan(gq, gk, gv, be, gce, gcr, kgt, batch, seq, 8)

        h = _mixffn(y_pool, y_attn, o_f, o_b, z, h, jnp.tile(gdn_norm[l], GDN_HEADS)[None, :], bd256,
                    w_out_b, post_mix_norm[:, None, :], pre_ffn_norm[:, None, :], w_up_b, ffn_conv, w_down_b,
                    post_ffn_norm[:, None, :], seq, tm_ffn, 256, l)
    return h.reshape(batch, seq, D_MODEL)
```

```python
import functools
import math

import numpy as np
import jax
import jax.numpy as jnp
from jax import lax
from jax.experimental import pallas as pl
from jax.experimental.pallas import tpu as pltpu

F32 = jnp.float32
BF16 = jnp.bfloat16

D_MODEL = 1024
GRID_W = 64
EPS = 1e-6
POOL_WIDTH = 256
POOL_WINDOWS = (2, 4, 8, 16)
POOL_GROUPS = 4
POOL_GDIM = POOL_WIDTH // POOL_GROUPS
ATTN_HEADS = 8
ATTN_KV_HEADS = 2
HEAD_DIM = 64
ATTN_WIDTH = ATTN_HEADS * HEAD_DIM
KV_WIDTH = ATTN_KV_HEADS * HEAD_DIM
ROPE_THETA = 10000.0
GDN_HEADS = 4
GDN_DK = 64
GDN_WIDTH = GDN_HEADS * GDN_DK
GDN_CONV = 5
GDN_CHUNK = 64
N_DIR = 2
D_FF = 2816
FFN_CONV = 3
D_IN = POOL_WIDTH + ATTN_WIDTH + 2 * KV_WIDTH + 4 * GDN_WIDTH + 2 * N_DIR * GDN_HEADS

LANES = 128
SUBLANES = 8
HALO = SUBLANES
VMEM_LIMIT = 56 * 1024 * 1024

N_CAT = 2176
COL_POOL = 0
COL_QA = COL_POOL + POOL_WIDTH
COL_KA = COL_QA + ATTN_WIDTH
COL_VA = COL_KA + KV_WIDTH
COL_GQKV = COL_VA + KV_WIDTH
COL_Z = COL_GQKV + 3 * GDN_WIDTH
COL_GATES = COL_Z + GDN_WIDTH

Q_SCALE = HEAD_DIM ** -0.5 * math.log2(math.e)
V_ROWS = HEAD_DIM + 16


def _dot(a, b):
    return jnp.dot(a, b, preferred_element_type=F32)


def _dot_nt(a, b):
    return lax.dot_general(a, b, (((1,), (1,)), ((), ())), preferred_element_type=F32)


def _split2(x):
    hi = x.astype(BF16)
    lo = (x - hi.astype(F32)).astype(BF16)
    return hi, lo


def _sel_dot(m01, x):
    hi, lo = _split2(x)
    return _dot(m01, hi) + _dot(m01, lo)


def _dot_sel(x, m01):
    hi, lo = _split2(x)
    return _dot(hi, m01) + _dot(lo, m01)


def _group_sum(x, bd):
    return _dot_sel(x, bd)


def _rms(x, w):
    ms = jnp.mean(x * x, axis=-1, keepdims=True)
    return x * lax.rsqrt(ms + EPS) * w


def _shift_rows(x, d):
    n = x.shape[0]
    return pltpu.roll(x, (-d) % n, 0)


def _params(sem):
    return pltpu.CompilerParams(dimension_semantics=sem, vmem_limit_bytes=VMEM_LIMIT)


def _halo_specs(tm, width, n_rows):
    per = tm // HALO
    last = n_rows // HALO - 1
    return [
        pl.BlockSpec((HALO, width), lambda i: (jnp.maximum(i * per - 1, 0), 0)),
        pl.BlockSpec((tm, width), lambda i: (i, 0)),
        pl.BlockSpec((HALO, width), lambda i: (jnp.minimum((i + 1) * per, last), 0)),
    ]


def _seq_edges(tiles_per_seq):
    t = pl.program_id(0) % tiles_per_seq
    return t, t == 0, t == tiles_per_seq - 1


def _inproj_kernel(xp_ref, x_ref, xnx_ref, nw_ref, w_ref, cos_ref, sin_ref, qw_ref, kw_ref, bd_ref,
                   bdw_ref, psc_ref, yp_ref, q_ref, k_ref, vt_ref, g_ref, z_ref, gt_ref, *, tps, seq):
    tm = x_ref.shape[0]
    xn = _rms(x_ref[...], nw_ref[...]).astype(BF16)

    cos = cos_ref[...]
    sin = sin_ref[...]
    lane = lax.broadcasted_iota(jnp.int32, (tm, LANES), 1)
    first_half = (lane & 16) == 0
    lo64 = lane < HEAD_DIM
    bd = bd_ref[...]

    def head_norm_rope(x, ss, w):
        y = x * lax.rsqrt(ss * (1.0 / HEAD_DIM) + EPS) * w
        partner = jnp.where(first_half, pltpu.roll(y, LANES - 16, 1), pltpu.roll(y, 16, 1))
        return y * cos + partner * sin

    n_slab = ATTN_WIDTH // LANES
    qa = _dot(xn, w_ref[:, COL_QA:COL_KA])
    ka = _dot(xn, w_ref[:, COL_KA:COL_VA])
    ss_q = [_group_sum(qa[:, p * 2 * LANES:(p + 1) * 2 * LANES] ** 2, bd) for p in range(n_slab // 2)]
    ss_q = [ss_q[s // 2][:, (s % 2) * LANES:(s % 2 + 1) * LANES] for s in range(n_slab)]
    qa = [qa[:, s * LANES:(s + 1) * LANES] for s in range(n_slab)]
    ss_k = _group_sum(ka * ka, bd[0:LANES, 0:LANES])

    t, first, last = _seq_edges(tps)
    w_pool = w_ref[:, COL_POOL:COL_QA]
    u = _dot(xn, w_pool)
    xh = _rms(jnp.concatenate([xp_ref[...], xnx_ref[...]], axis=0), nw_ref[...]).astype(BF16)
    uh = _dot(xh, w_pool)
    ext = jnp.concatenate([jnp.where(first, 0.0, uh[0:HALO]), u, jnp.where(last, 0.0, uh[HALO:2 * HALO])], axis=0)
    w2 = ext + _shift_rows(ext, -1)
    w4 = _shift_rows(w2, -1) + _shift_rows(w2, 1)
    w8 = _shift_rows(w4, -2) + _shift_rows(w4, 2)
    w16 = _shift_rows(w8, -4) + _shift_rows(w8, 4)
    sl = slice(HALO, HALO + tm)
    grp = lax.broadcasted_iota(jnp.int32, (tm, POOL_WIDTH), 1) // POOL_GDIM
    seg = jnp.where(grp == 0, w2[sl], jnp.where(grp == 1, w4[sl], jnp.where(grp == 2, w8[sl], w16[sl])))
    half = jnp.where(grp == 0, 1, jnp.where(grp == 1, 2, jnp.where(grp == 2, 4, 8)))
    pos = t * tm + lax.broadcasted_iota(jnp.int32, (tm, POOL_WIDTH), 0)
    cnt = (jnp.minimum(pos + half, seq) - jnp.maximum(pos - half, 0)).astype(F32)
    pooled = (seg / cnt - u).astype(BF16)

    qw = qw_ref[...] * Q_SCALE
    q_rot = jnp.concatenate([head_norm_rope(qa[s], ss_q[s], qw) for s in range(n_slab)], axis=1)
    q_ref[...] = q_rot.T.astype(BF16)

    kr = head_norm_rope(ka, ss_k, kw_ref[...])
    ks = pltpu.roll(kr, HEAD_DIM, 1)
    k_ref[:, 0:128] = jnp.where(lo64, kr, 0.0).astype(BF16)
    k_ref[:, 128:256] = jnp.where(lo64, 0.0, ks).astype(BF16)
    k_ref[:, 256:384] = jnp.where(lo64, ks, 0.0).astype(BF16)
    k_ref[:, 384:512] = jnp.where(lo64, 0.0, kr).astype(BF16)

    vat = _dot(xn, w_ref[:, COL_VA:COL_GQKV]).T
    pad_row = lax.broadcasted_iota(jnp.int32, (V_ROWS - HEAD_DIM, tm), 0)
    ones_rows = jnp.where(pad_row == 0, 1.0, 0.0)
    vt_ref[...] = jnp.concatenate(
        [vat[0:HEAD_DIM], ones_rows, vat[HEAD_DIM:2 * HEAD_DIM], ones_rows], axis=0).astype(BF16)
    g_ref[...] = _dot(xn, w_ref[:, COL_GQKV:COL_Z])
    z_ref[...] = _dot(xn, w_ref[:, COL_Z:COL_GATES])
    gt_ref[...] = _dot(xn, w_ref[:, COL_GATES:N_CAT])
    yp_ref[...] = (_dot(pooled, bdw_ref[...]) * psc_ref[...]).astype(BF16)


def _inproj(x2d, norm_w, w_cat, cos_t, sin_t, qw, kw, bd256, bdw, pool_scale, seq, tm, layer):
    n_tok = x2d.shape[0]
    tps = seq // tm
    const = lambda i: (0, 0)
    row = lambda i: (i, 0)
    pos = lambda i: (i % tps, 0)
    return pl.pallas_call(
        functools.partial(_inproj_kernel, tps=tps, seq=seq),
        grid=(n_tok // tm,),
        in_specs=_halo_specs(tm, D_MODEL, n_tok) + [
            pl.BlockSpec((None, 1, D_MODEL), lambda i: (layer, 0, 0)),
            pl.BlockSpec((None, D_MODEL, N_CAT), lambda i: (layer, 0, 0)),
            pl.BlockSpec((tm, LANES), pos),
            pl.BlockSpec((tm, LANES), pos),
            pl.BlockSpec((1, LANES), const),
            pl.BlockSpec((1, LANES), const),
            pl.BlockSpec((2 * LANES, 2 * LANES), const),
            pl.BlockSpec((POOL_WIDTH, POOL_WIDTH), const),
            pl.BlockSpec((1, POOL_WIDTH), const),
        ],
        out_specs=[
            pl.BlockSpec((tm, POOL_WIDTH), row),
            pl.BlockSpec((ATTN_WIDTH, tm), lambda i: (0, i)),
            pl.BlockSpec((tm, 4 * LANES), row),
            pl.BlockSpec((ATTN_KV_HEADS * V_ROWS, tm), lambda i: (0, i)),
            pl.BlockSpec((tm, 3 * GDN_WIDTH), row),
            pl.BlockSpec((tm, GDN_WIDTH), row),
            pl.BlockSpec((tm, LANES), row),
        ],
        out_shape=[
            jax.ShapeDtypeStruct((n_tok, POOL_WIDTH), BF16),
            jax.ShapeDtypeStruct((ATTN_WIDTH, n_tok), BF16),
            jax.ShapeDtypeStruct((n_tok, 4 * LANES), BF16),
            jax.ShapeDtypeStruct((ATTN_KV_HEADS * V_ROWS, n_tok), BF16),
            jax.ShapeDtypeStruct((n_tok, 3 * GDN_WIDTH), F32),
            jax.ShapeDtypeStruct((n_tok, GDN_WIDTH), F32),
            jax.ShapeDtypeStruct((n_tok, LANES), F32),
        ],
        compiler_params=_params(("parallel",)),
        name="inproj",
    )(x2d, x2d, x2d, norm_w, w_cat, cos_t, sin_t, qw, kw, bd256, bdw, pool_scale)


def _attn_kernel(q_ref, qn_ref, k_ref, vt_ref, o_ref, acc_ref, m_ref, s_ref, bm_ref, qq_ref, *, tk, per_iter):
    seq = k_ref.shape[0]
    group = ATTN_HEADS // ATTN_KV_HEADS
    n_chunks = seq // tk
    n_iter = n_chunks // per_iter
    acc_ref[...] = jnp.zeros_like(acc_ref)
    m_ref[...] = jnp.full_like(m_ref, -1e30)
    qq_ref[0] = q_ref[...]
    qq_ref[1] = qn_ref[...]

    def scores(c, buf, slab, tile=0):
        r0 = pl.multiple_of(c * tk, tk)
        kk = jnp.concatenate([k_ref[pl.ds(r0, tk), 0:LANES], k_ref[pl.ds(r0, tk), LANES:2 * LANES]], axis=0)
        s2 = _dot(kk, qq_ref[tile, slab * LANES:(slab + 1) * LANES, :])
        for half in range(2):
            j = 2 * slab + half
            s = s2[half * tk:(half + 1) * tk]
            s_ref[buf, j] = s
            bm_ref[buf, j] = jnp.max(s, axis=0, keepdims=True)

    def softmax_pv(c, buf, slab):
        r0 = pl.multiple_of(c * tk, tk)
        vt = vt_ref[:, pl.ds(r0, tk)]
        for j in (2 * slab, 2 * slab + 1):
            m = m_ref[j]
            m_new = jnp.maximum(m, bm_ref[buf, j])
            alpha = jnp.exp2(m - m_new)
            p = jnp.exp2(s_ref[buf, j] - m_new)
            m_ref[j] = m_new
            acc_ref[j] = alpha * acc_ref[j] + _dot(vt, p.astype(BF16))

    slabs = range(group // 2)

    @pl.when(pl.program_id(2) == 0)
    def _():
        for slab in slabs:
            scores(0, 0, slab)

    def body(i, carry):
        c0 = per_iter * i
        wrap = jnp.where(i == n_iter - 1, 1, 0)
        for u in range(per_iter):
            for slab in slabs:
                if u == per_iter - 1:
                    scores((1 - wrap) * (c0 + per_iter), 0, slab, tile=wrap)
                else:
                    scores(c0 + u + 1, (u + 1) % 2, slab)
            for slab in slabs:
                softmax_pv(c0 + u, u % 2, slab)
        return carry

    lax.fori_loop(0, n_iter, body, 0)
    out = jnp.concatenate([acc_ref[j, 0:HEAD_DIM] / acc_ref[j, HEAD_DIM:HEAD_DIM + 1] for j in range(group)], axis=0)
    o_ref[...] = out.T.astype(BF16)


def _attention(qt, k4, vt, batch, seq, tq, tk, per_iter):
    n_tok = k4.shape[0]
    nq = seq // tq
    group = ATTN_HEADS // ATTN_KV_HEADS
    gw = group * HEAD_DIM
    per_iter = min(per_iter, seq // tk)
    assert per_iter % 2 == 0 and (seq // tk) % per_iter == 0
    return pl.pallas_call(
        functools.partial(_attn_kernel, tk=tk, per_iter=per_iter),
        grid=(batch, ATTN_KV_HEADS, nq),
        in_specs=[
            pl.BlockSpec((gw, tq), lambda b, g, i: (g, b * nq + i)),
            pl.BlockSpec((gw, tq), lambda b, g, i: (g, b * nq + jnp.minimum(i + 1, nq - 1))),
            pl.BlockSpec((seq, 2 * LANES), lambda b, g, i: (b, g)),
            pl.BlockSpec((V_ROWS, seq), lambda b, g, i: (g, b)),
        ],
        out_specs=pl.BlockSpec((tq, gw), lambda b, g, i: (b * nq + i, g)),
        out_shape=jax.ShapeDtypeStruct((n_tok, ATTN_WIDTH), BF16),
        scratch_shapes=[pltpu.VMEM((group, V_ROWS, tq), F32),
                        pltpu.VMEM((group, 1, tq), F32),
                        pltpu.VMEM((2, group, tk, tq), F32),
                        pltpu.VMEM((2, group, 1, tq), F32),
                        pltpu.VMEM((2, gw, tq), BF16)],
        compiler_params=_params(("parallel", "parallel", "arbitrary")),
        name="attention",
    )(qt, qt, k4, vt)


def _gdn_prep_kernel(prev_ref, x_ref, next_ref, gt_ref, cw_ref, na_ref, dtb_ref, bd_ref, ex_ref,
                     q_ref, k_ref, v_ref, be_ref, gce_ref, gcr_ref, kgt_ref, *, tps):
    tm = x_ref.shape[0]
    _, first, last = _seq_edges(tps)
    prev = jnp.where(first, 0.0, prev_ref[...])
    nxt = jnp.where(last, 0.0, next_ref[...])
    ext = jnp.concatenate([prev, x_ref[...], nxt], axis=0)
    cw = cw_ref[...]
    conv = ext * cw[2:3]
    for kk in (0, 1, 3, 4):
        conv = conv + _shift_rows(ext, kk - GDN_CONV // 2) * cw[kk:kk + 1]
    conv = conv[HALO:HALO + tm]
    act = conv / (1.0 + jnp.exp(-conv))
    bd = bd_ref[...]

    def l2n(x):
        return x * lax.rsqrt(_group_sum(x * x, bd) + EPS)

    q = l2n(act[:, 0:GDN_WIDTH]) * (GDN_DK ** -0.5)
    k = l2n(act[:, GDN_WIDTH:2 * GDN_WIDTH])
    q_ref[...] = q
    k_ref[...] = k
    v_ref[...] = act[:, 2 * GDN_WIDTH:3 * GDN_WIDTH]

    gt = gt_ref[...]
    lane = lax.broadcasted_iota(jnp.int32, (tm, LANES), 1)
    beta = 1.0 / (1.0 + jnp.exp(-gt))
    xs = gt + dtb_ref[...]
    softplus = jnp.maximum(xs, 0.0) + jnp.log1p(jnp.exp(-jnp.abs(xs)))
    g = na_ref[...] * softplus

    r = lax.broadcasted_iota(jnp.int32, (tm, tm), 0)
    c = lax.broadcasted_iota(jnp.int32, (tm, tm), 1)
    same = (r // GDN_CHUNK) == (c // GDN_CHUNK)
    ones_bd = jnp.where(same, 1.0, 0.0).astype(BF16)
    tri_f = jnp.where(same & (c <= r), 1.0, 0.0).astype(BF16)
    tri_b = jnp.where(same & (c >= r), 1.0, 0.0).astype(BF16)
    gc = jnp.where(lane < 12, _sel_dot(tri_f, g), _sel_dot(tri_b, g))
    comp = jnp.where(lane < 8, beta, gc)
    exp_all = _dot_sel(comp, ex_ref[...])
    tot = _sel_dot(ones_bd, g)
    tot_e = _dot_sel(tot, ex_ref[:, 2 * GDN_WIDTH:4 * GDN_WIDTH])

    row = lax.broadcasted_iota(jnp.int32, (tm, GDN_WIDTH), 0)
    col = lax.broadcasted_iota(jnp.int32, (tm, GDN_WIDTH), 1)
    diag = (row % GDN_CHUNK) == (col % GDN_DK)
    for d in range(N_DIR):
        be_ref[d] = exp_all[:, d * GDN_WIDTH:(d + 1) * GDN_WIDTH]
        gce = exp_all[:, (2 + d) * GDN_WIDTH:(3 + d) * GDN_WIDTH]
        gce_ref[d] = gce
        gcr_ref[d] = _sel_dot(ones_bd, jnp.where(diag, gce, 0.0))
        kg = k * jnp.exp(tot_e[:, d * GDN_WIDTH:(d + 1) * GDN_WIDTH] - gce)
        lo_lane = lax.broadcasted_iota(jnp.int32, (GDN_DK, LANES), 1) < GDN_DK
        for pr in range(tm // LANES):
            xt = kg[pr * LANES:(pr + 1) * LANES, :].T
            xr = pltpu.roll(xt, GDN_CHUNK, 1)
            for cl in range(2):
                even, odd = (xt, xr) if cl == 0 else (xr, xt)
                tiles = [jnp.where(lo_lane, even[(2 * p) * GDN_DK:(2 * p + 1) * GDN_DK],
                                   odd[(2 * p + 1) * GDN_DK:(2 * p + 2) * GDN_DK]) for p in range(GDN_HEADS // 2)]
                r0 = (2 * pr + cl) * GDN_CHUNK
                kgt_ref[d, r0:r0 + GDN_CHUNK, :] = jnp.concatenate(tiles, axis=1).astype(BF16)


def _gdn_prep(gqkv, gates, conv_w, neg_a, dt_b, bd256, expand, seq, tm):
    n_tok = gqkv.shape[0]
    width = 3 * GDN_WIDTH
    const = lambda i: (0, 0)
    row = lambda i: (i, 0)
    drow = lambda i: (0, i, 0)
    tok = jax.ShapeDtypeStruct((n_tok, GDN_WIDTH), F32)
    dtok = jax.ShapeDtypeStruct((N_DIR, n_tok, GDN_WIDTH), F32)
    return pl.pallas_call(
        functools.partial(_gdn_prep_kernel, tps=seq // tm),
        grid=(n_tok // tm,),
        in_specs=_halo_specs(tm, width, n_tok) + [
            pl.BlockSpec((tm, LANES), row),
            pl.BlockSpec((GDN_CONV, width), const),
            pl.BlockSpec((1, LANES), const),
            pl.BlockSpec((1, LANES), const),
            pl.BlockSpec((GDN_WIDTH, GDN_WIDTH), const),
            pl.BlockSpec((LANES, 4 * GDN_WIDTH), const),
        ],
        out_specs=[
            pl.BlockSpec((tm, GDN_WIDTH), row),
            pl.BlockSpec((tm, GDN_WIDTH), row),
            pl.BlockSpec((tm, GDN_WIDTH), row),
            pl.BlockSpec((N_DIR, tm, GDN_WIDTH), drow),
            pl.BlockSpec((N_DIR, tm, GDN_WIDTH), drow),
            pl.BlockSpec((N_DIR, tm, GDN_WIDTH), drow),
            pl.BlockSpec((N_DIR, tm, GDN_WIDTH), drow),
        ],
        out_shape=[tok, tok, tok, dtok, dtok, dtok,
                   jax.ShapeDtypeStruct((N_DIR, n_tok, GDN_WIDTH), BF16)],
        compiler_params=_params(("parallel",)),
        name="gdn_prep",
    )(gqkv, gqkv, gqkv, gates, conv_w, neg_a, dt_b, bd256, expand)


def _gdn_scan_kernel(qf_ref, kf_ref, vf_ref, bef_ref, gcef_ref, gcrf_ref,
                     qb_ref, kb_ref, vb_ref, beb_ref, gceb_ref, gcrb_ref, kgpf_ref, kgpb_ref,
                     of_ref, ob_ref, state_ref, stage_ref, gl_ref, *, chunks, wave):
    cs = GDN_CHUNK
    w = GDN_WIDTH
    par = pl.program_id(1) % 2
    prv = 1 - par

    @pl.when(pl.program_id(1) == 0)
    def _():
        state_ref[...] = jnp.zeros_like(state_ref)
        stage_ref[...] = jnp.zeros_like(stage_ref)
        gl_ref[...] = jnp.zeros_like(gl_ref)

    lane_h = lax.broadcasted_iota(jnp.int32, (cs, w), 1) // GDN_DK
    heads = [lane_h == h for h in range(GDN_HEADS)]
    ti = lax.broadcasted_iota(jnp.int32, (cs, w), 0)
    tj = lax.broadcasted_iota(jnp.int32, (cs, w), 1) % cs
    eye = (ti == tj).astype(F32)
    in_refs = ((qf_ref, kf_ref, vf_ref, bef_ref, gcef_ref, gcrf_ref),
               (qb_ref, kb_ref, vb_ref, beb_ref, gceb_ref, gcrb_ref))
    kgp_refs = (kgpf_ref, kgpb_ref)
    o_refs = (of_ref, ob_ref)
    U, W, QG, QKM = range(4)

    def blockdiag(y):
        return jnp.concatenate([jnp.where(hm, y, 0.0) for hm in heads], axis=0).astype(BF16)

    def rows_of(ci):
        return slice(ci * cs, (ci + 1) * cs)

    loc = {}

    def local_scores(d, ci):
        q_ref, k_ref, v_ref, be_ref, gce_ref, gcr_ref = in_refs[d]
        rs = rows_of(ci)
        q = q_ref[rs, :]
        k = k_ref[rs, :]
        be = be_ref[0, rs, :]
        kb = k * be
        kq = _dot_nt(jnp.concatenate([kb, q], axis=0).astype(BF16), blockdiag(k))
        loc[d, ci] = dict(kq=kq, kb=kb)

    def local_decay(d, ci):
        st = loc[d, ci]
        _, _, _, _, gce_ref, gcr_ref = in_refs[d]
        rs = rows_of(ci)
        incl = (ti >= tj) if d == 0 else (ti <= tj)
        strict = (ti > tj) if d == 0 else (ti < tj)
        diff = gce_ref[0, rs, :] - gcr_ref[0, rs, :]
        decay = jnp.where(incl, jnp.exp(jnp.where(incl, diff, 0.0)), 0.0)
        kq = st.pop("kq")
        a = jnp.where(strict, kq[0:cs] * decay, 0.0)
        stage_ref[par, d, QKM, rs, :] = kq[cs:2 * cs] * decay
        st["pw"] = _dot(a.astype(BF16), blockdiag(a))
        st["t"] = eye - a

    def local_invert(d, ci, last):
        st = loc[d, ci]
        bdp = blockdiag(st["pw"])
        if last:
            st["t"] = st["t"] + _dot(st["t"].astype(BF16), bdp)
        else:
            both = _dot(jnp.concatenate([st["pw"], st["t"]], axis=0).astype(BF16), bdp)
            st["pw"] = both[0:cs]
            st["t"] = st["t"] + both[cs:2 * cs]

    def local_uw(d, ci):
        st = loc.pop((d, ci))
        q_ref, _, v_ref, be_ref, gce_ref, _ = in_refs[d]
        rs = rows_of(ci)
        eg = jnp.exp(gce_ref[0, rs, :])
        vb = v_ref[rs, :] * be_ref[0, rs, :]
        rhs = jnp.concatenate([blockdiag(vb), blockdiag(st["kb"] * eg)], axis=1)
        uw = _dot(st["t"].astype(BF16), rhs)
        stage_ref[par, d, U, rs, :] = uw[:, 0:w]
        stage_ref[par, d, W, rs, :] = uw[:, w:2 * w]
        stage_ref[par, d, QG, rs, :] = q_ref[rs, :] * eg
        edge = (ci + 1) * cs - 1 if d == 0 else ci * cs
        gl_ref[par, d, ci] = jnp.exp(gce_ref[0, edge:edge + 1, :])

    scan = {}

    def scan_read(d, ci):
        rs = rows_of(ci)
        lhs = jnp.concatenate([stage_ref[prv, d, W, rs, :], stage_ref[prv, d, QG, rs, :]], axis=0)
        scan[d] = _dot(lhs.astype(BF16), blockdiag(state_ref[d]))

    def scan_out(d, ci):
        rs = rows_of(ci)
        ws = scan[d]
        vn = stage_ref[prv, d, U, rs, :] - ws[0:cs]
        lhs = jnp.concatenate([stage_ref[prv, d, QKM, rs, :].astype(BF16), kgp_refs[d][0, rs, :]], axis=0)
        res = _dot(lhs, blockdiag(vn))
        o_refs[d][rs, :] = ws[cs:2 * cs] + res[0:cs]
        state_ref[d] = state_ref[d] * gl_ref[prv, d, ci] + res[cs:2 * cs]

    order = [(d, ci if d == 0 else chunks - 1 - ci) for ci in range(chunks) for d in range(N_DIR)]
    local_ops = []
    for w0 in range(0, len(order), wave):
        wv = order[w0:w0 + wave]
        local_ops += [functools.partial(local_scores, d, ci) for d, ci in wv]
        local_ops += [functools.partial(local_decay, d, ci) for d, ci in wv]
        for step in range(5):
            local_ops += [functools.partial(local_invert, d, ci, step == 4) for d, ci in wv]
        local_ops += [functools.partial(local_uw, d, ci) for d, ci in wv]
    scan_ops = []
    for ci in range(chunks):
        for fn in (scan_read, scan_out):
            scan_ops.append([functools.partial(fn, d, ci if d == 0 else chunks - 1 - ci) for d in range(N_DIR)])
    every = len(local_ops) // len(scan_ops)
    for idx, op in enumerate(local_ops):
        op()
        if (idx + 1) % every == 0 and scan_ops:
            for s_op in scan_ops.pop(0):
                s_op()
    for level in scan_ops:
        for s_op in level:
            s_op()


def _gdn_scan(q, k, v, be, gce, gcr, kgp, batch, seq, chunks):
    n_tok = q.shape[0]
    rows = chunks * GDN_CHUNK
    ng = seq // rows
    w = GDN_WIDTH

    def group(d, lag):
        def f(b, n):
            g = jnp.clip(n - lag, 0, ng - 1)
            return b * ng + (ng - 1 - g if d == 1 else g)
        return f

    def tok(d, lag):
        f = group(d, lag)
        return lambda b, n: (f(b, n), 0)

    def dtok(d, lag):
        f = group(d, lag)
        return lambda b, n: (d, f(b, n), 0)

    def local_specs(d):
        return [pl.BlockSpec((rows, w), tok(d, 0))] * 3 + [pl.BlockSpec((1, rows, w), dtok(d, 0))] * 3

    out = jax.ShapeDtypeStruct((n_tok, w), F32)
    return pl.pallas_call(
        functools.partial(_gdn_scan_kernel, chunks=chunks, wave=2 * chunks),
        grid=(batch, ng + 1),
        in_specs=local_specs(0) + local_specs(1) + [pl.BlockSpec((1, rows, w), dtok(d, 1)) for d in range(N_DIR)],
        out_specs=[pl.BlockSpec((rows, w), tok(d, 1)) for d in range(N_DIR)],
        out_shape=[out, out],
        scratch_shapes=[pltpu.VMEM((N_DIR, GDN_DK, w), F32),
                        pltpu.VMEM((2, N_DIR, 4, rows, w), F32),
                        pltpu.VMEM((2, N_DIR, chunks, 1, w), F32)],
        compiler_params=_params(("parallel", "arbitrary")),
        name="gdn_scan",
    )(q, k, v, be, gce, gcr, q, k, v, be, gce, gcr, kgp, kgp)


def _mixffn_kernel(yp_p, yp_m, yp_n, ya_p, ya_m, ya_n, of_p, of_m, of_n, ob_p, ob_m, ob_n,
                   z_p, z_m, z_n, h_p, h_m, h_n, gw_ref, bd_ref, wo_ref, postmix_ref,
                   prew_ref, wg_ref, wv_ref, cg_ref, cv_ref, wd_ref, postw_ref, o_ref, act_ref, *, tps, chunk):
    tm = h_m.shape[0]
    _, first, last = _seq_edges(tps)

    def ext(p_ref, m_ref, n_ref):
        p = p_ref[...].astype(F32)
        n = n_ref[...].astype(F32)
        return jnp.concatenate([p[p.shape[0] - HALO:], m_ref[...].astype(F32), n[0:HALO]], axis=0)

    a0 = POOL_WIDTH
    a1 = POOL_WIDTH + ATTN_WIDTH
    mix = _dot(ext(yp_p, yp_m, yp_n).astype(BF16), wo_ref[0:a0, :])
    mix = mix + _dot(ext(ya_p, ya_m, ya_n).astype(BF16), wo_ref[a0:a1, :])
    o = ext(of_p, of_m, of_n) + ext(ob_p, ob_m, ob_n)
    ms = _group_sum(o * o, bd_ref[...]) * (1.0 / GDN_DK)
    z = ext(z_p, z_m, z_n)
    yg = o * lax.rsqrt(ms + EPS) * gw_ref[...] * (z / (1.0 + jnp.exp(-z)))
    mix = mix + _dot(yg.astype(BF16), wo_ref[a1:D_MODEL, :])
    h1 = ext(h_p, h_m, h_n) + _rms(mix, postmix_ref[...])

    row = lax.broadcasted_iota(jnp.int32, (tm + 2 * HALO, 1), 0)
    outside = (first & (row < HALO)) | (last & (row >= HALO + tm))
    xn = jnp.where(outside, 0.0, _rms(h1, prew_ref[...])).astype(BF16)
    sl = slice(HALO, HALO + tm)

    def conv3(up, cw):
        out = _shift_rows(up, -1) * cw[0:1] + up * cw[1:2] + _shift_rows(up, 1) * cw[2:3]
        return out[sl]

    for c in range(D_FF // chunk):
        cs = slice(c * chunk, (c + 1) * chunk)
        gate = conv3(_dot(xn, wg_ref[:, cs]), cg_ref[:, cs])
        val = conv3(_dot(xn, wv_ref[:, cs]), cv_ref[:, cs])
        inner = 0.7978845608028654 * (gate + 0.044715 * (gate * gate * gate))
        act_ref[:, cs] = (0.5 * gate * (1.0 + jnp.tanh(inner)) * val).astype(BF16)
    ff = _dot(act_ref[...], wd_ref[...])
    o_ref[...] = h1[sl] + _rms(ff, postw_ref[...])


def _mixffn(y_pool, y_attn, o_f, o_b, z, h, gdn_nw, bd256, w_out, post_mix_w,
            pre_w, w_up, conv_w, w_down, post_w, seq, tm, chunk, layer):
    n_tok = h.shape[0]
    const = lambda i: (0, 0)
    mat = lambda i: (layer, 0, 0)
    second = lambda i: (layer, 0, 1)
    resident = pl.Buffered(1)

    def halo(width, rows):
        per = tm // rows
        last = n_tok // rows - 1
        return [
            pl.BlockSpec((rows, width), lambda i: (jnp.maximum(i * per - 1, 0), 0)),
            pl.BlockSpec((tm, width), lambda i: (i, 0)),
            pl.BlockSpec((rows, width), lambda i: (jnp.minimum((i + 1) * per, last), 0)),
        ]

    bf16_rows = 2 * SUBLANES
    in_specs = (halo(POOL_WIDTH, bf16_rows) + halo(ATTN_WIDTH, bf16_rows) + halo(GDN_WIDTH, HALO) * 3
                + halo(D_MODEL, HALO) + [
        pl.BlockSpec((1, GDN_WIDTH), const),
        pl.BlockSpec((GDN_WIDTH, GDN_WIDTH), const),
        pl.BlockSpec((None, D_MODEL, D_MODEL), mat, pipeline_mode=resident),
        pl.BlockSpec((None, 1, D_MODEL), mat),
        pl.BlockSpec((None, 1, D_MODEL), mat),
        pl.BlockSpec((None, D_MODEL, D_FF), mat, pipeline_mode=resident),
        pl.BlockSpec((None, D_MODEL, D_FF), second, pipeline_mode=resident),
        pl.BlockSpec((None, FFN_CONV, D_FF), mat),
        pl.BlockSpec((None, FFN_CONV, D_FF), second),
        pl.BlockSpec((None, D_FF, D_MODEL), mat, pipeline_mode=resident),
        pl.BlockSpec((None, 1, D_MODEL), mat),
    ])
    return pl.pallas_call(
        functools.partial(_mixffn_kernel, tps=seq // tm, chunk=chunk),
        grid=(n_tok // tm,),
        in_specs=in_specs,
        out_specs=pl.BlockSpec((tm, D_MODEL), lambda i: (i, 0)),
        out_shape=jax.ShapeDtypeStruct((n_tok, D_MODEL), F32),
        scratch_shapes=[pltpu.VMEM((tm, D_FF), BF16)],
        compiler_params=_params(("parallel",)),
        name="mixffn",
    )(y_pool, y_pool, y_pool, y_attn, y_attn, y_attn, o_f, o_f, o_f, o_b, o_b, o_b, z, z, z, h, h, h,
      gdn_nw, bd256, w_out, post_mix_w, pre_w, w_up, w_up, conv_w, conv_w, w_down, post_w)


def _rope_tables(seq):
    t = np.arange(seq)
    pos = np.stack([t // GRID_W, t % GRID_W], axis=-1).astype(np.float32)
    axis_dim = HEAD_DIM // 2
    inv_freq = (ROPE_THETA ** (-np.arange(0, axis_dim, 2, dtype=np.float32) / axis_dim)).astype(np.float32)
    ang = pos[:, :, None] * inv_freq
    j = np.arange(LANES) % HEAD_DIM
    a = ang[:, j // 32, j % 16]
    sign = np.where((j % 32) < 16, -1.0, 1.0).astype(np.float32)
    return jnp.asarray(np.cos(a), F32), jnp.asarray(np.sin(a) * sign, F32)


def _head_blockdiag(n):
    i = np.arange(n) // HEAD_DIM
    return jnp.asarray(i[:, None] == i[None, :], BF16)


def _gate_expand():
    e = np.zeros((LANES, 4 * GDN_WIDTH), np.float32)
    for kind in range(2):
        for d in range(N_DIR):
            for hh in range(GDN_HEADS):
                src = kind * N_DIR * GDN_HEADS + d * GDN_HEADS + hh
                dst = (kind * N_DIR + d) * GDN_WIDTH + hh * GDN_DK
                e[src, dst:dst + GDN_DK] = 1.0
    return jnp.asarray(e, BF16)


def _pad_lanes(v, offset):
    out = jnp.zeros((1, LANES), F32)
    return out.at[0, offset:offset + v.shape[0]].set(v.astype(F32))


def kernel(x, pre_mix_norm, post_mix_norm, pre_ffn_norm, post_ffn_norm, w_in, pool_w, pool_scale, q_norm, k_norm, gdn_conv, gdn_a_log, gdn_dt_bias, gdn_norm, w_out, w_up, ffn_conv, w_down):
    batch, seq, _ = x.shape
    n_tok = batch * seq
    depth = w_in.shape[0]
    assert seq % 512 == 0 and seq % GRID_W == 0

    tm_proj = 512
    tm_gdn = 256
    tm_ffn = 512
    cos_t, sin_t = _rope_tables(seq)
    bd256 = _head_blockdiag(GDN_WIDTH)
    expand = _gate_expand()
    n_gate = N_DIR * GDN_HEADS

    w_cat = jnp.pad(w_in, ((0, 0), (0, 0), (0, N_CAT - D_IN))).astype(BF16)
    w_out_b = w_out.astype(BF16)
    w_up_b = w_up.astype(BF16)
    w_down_b = w_down.astype(BF16)

    h = x.reshape(n_tok, D_MODEL)
    for l in range(depth):
        qw = jnp.tile(q_norm[l], 2)[None, :]
        kw = jnp.tile(k_norm[l], 2)[None, :]
        bdw = jax.scipy.linalg.block_diag(*[pool_w[l, g] for g in range(POOL_GROUPS)]).astype(BF16)
        y_pool, qt, k4, vt, gqkv, z, gates = _inproj(
            h, pre_mix_norm[:, None, :], w_cat, cos_t, sin_t, qw, kw, bd256, bdw, pool_scale[l][None, :],
            seq, tm_proj, l)

        y_attn = _attention(qt, k4, vt, batch, seq, 256, 256, 16)

        neg_a = _pad_lanes(-jnp.exp(gdn_a_log[l].reshape(-1)), n_gate)
        dt_b = _pad_lanes(gdn_dt_bias[l].reshape(-1), n_gate)
        gq, gk, gv, be, gce, gcr, kgt = _gdn_prep(gqkv, gates, gdn_conv[l], neg_a, dt_b, bd256, expand, seq, tm_gdn)
        o_f, o_b = _gdn_scan(gq, gk, gv, be, gce, gcr, kgt, batch, seq, 8)

        h = _mixffn(y_pool, y_attn, o_f, o_b, z, h, jnp.tile(gdn_norm[l], GDN_HEADS)[None, :], bd256,
                    w_out_b, post_mix_norm[:, None, :], pre_ffn_norm[:, None, :], w_up_b, ffn_conv, w_down_b,
                    post_ffn_norm[:, None, :], seq, tm_ffn, 256, l)
    return h.reshape(batch, seq, D_MODEL)
```

```python
import functools
import math

import numpy as np
import jax
import jax.numpy as jnp
from jax import lax
from jax.experimental import pallas as pl
from jax.experimental.pallas import tpu as pltpu

F32 = jnp.float32
BF16 = jnp.bfloat16

D_MODEL = 1024
GRID_W = 64
EPS = 1e-6
POOL_WIDTH = 256
POOL_WINDOWS = (2, 4, 8, 16)
POOL_GROUPS = 4
POOL_GDIM = POOL_WIDTH // POOL_GROUPS
ATTN_HEADS = 8
ATTN_KV_HEADS = 2
HEAD_DIM = 64
ATTN_WIDTH = ATTN_HEADS * HEAD_DIM
KV_WIDTH = ATTN_KV_HEADS * HEAD_DIM
ROPE_THETA = 10000.0
GDN_HEADS = 4
GDN_DK = 64
GDN_WIDTH = GDN_HEADS * GDN_DK
GDN_CONV = 5
GDN_CHUNK = 64
N_DIR = 2
D_FF = 2816
FFN_CONV = 3
D_IN = POOL_WIDTH + ATTN_WIDTH + 2 * KV_WIDTH + 4 * GDN_WIDTH + 2 * N_DIR * GDN_HEADS

LANES = 128
SUBLANES = 8
HALO = SUBLANES
VMEM_LIMIT = 56 * 1024 * 1024

N_CAT = 2176
COL_POOL = 0
COL_QA = COL_POOL + POOL_WIDTH
COL_KA = COL_QA + ATTN_WIDTH
COL_VA = COL_KA + KV_WIDTH
COL_GQKV = COL_VA + KV_WIDTH
COL_Z = COL_GQKV + 3 * GDN_WIDTH
COL_GATES = COL_Z + GDN_WIDTH

Q_SCALE = HEAD_DIM ** -0.5 * math.log2(math.e)
V_ROWS = HEAD_DIM + 16


def _dot(a, b):
    return jnp.dot(a, b, preferred_element_type=F32)


def _dot_nt(a, b):
    return lax.dot_general(a, b, (((1,), (1,)), ((), ())), preferred_element_type=F32)


def _split2(x):
    hi = x.astype(BF16)
    lo = (x - hi.astype(F32)).astype(BF16)
    return hi, lo


def _sel_dot(m01, x):
    hi, lo = _split2(x)
    return _dot(m01, hi) + _dot(m01, lo)


def _dot_sel(x, m01):
    hi, lo = _split2(x)
    return _dot(hi, m01) + _dot(lo, m01)


def _group_sum(x, bd):
    return _dot_sel(x, bd)


def _rms(x, w):
    ms = jnp.mean(x * x, axis=-1, keepdims=True)
    return x * lax.rsqrt(ms + EPS) * w


def _shift_rows(x, d):
    n = x.shape[0]
    return pltpu.roll(x, (-d) % n, 0)


def _params(sem):
    return pltpu.CompilerParams(dimension_semantics=sem, vmem_limit_bytes=VMEM_LIMIT)


def _halo_specs(tm, width, n_rows):
    per = tm // HALO
    last = n_rows // HALO - 1
    return [
        pl.BlockSpec((HALO, width), lambda i: (jnp.maximum(i * per - 1, 0), 0)),
        pl.BlockSpec((tm, width), lambda i: (i, 0)),
        pl.BlockSpec((HALO, width), lambda i: (jnp.minimum((i + 1) * per, last), 0)),
    ]


def _seq_edges(tiles_per_seq):
    t = pl.program_id(0) % tiles_per_seq
    return t, t == 0, t == tiles_per_seq - 1


def _inproj_kernel(xp_ref, x_ref, xnx_ref, nw_ref, w_ref, cos_ref, sin_ref, qw_ref, kw_ref, bd_ref,
                   bdw_ref, psc_ref, yp_ref, q_ref, k_ref, vt_ref, g_ref, z_ref, gt_ref, *, tps, seq):
    tm = x_ref.shape[0]
    xn = _rms(x_ref[...], nw_ref[...]).astype(BF16)

    cos = cos_ref[...]
    sin = sin_ref[...]
    lane = lax.broadcasted_iota(jnp.int32, (tm, LANES), 1)
    first_half = (lane & 16) == 0
    lo64 = lane < HEAD_DIM
    bd = bd_ref[...]

    def head_norm_rope(x, ss, w):
        y = x * lax.rsqrt(ss * (1.0 / HEAD_DIM) + EPS) * w
        partner = jnp.where(first_half, pltpu.roll(y, LANES - 16, 1), pltpu.roll(y, 16, 1))
        return y * cos + partner * sin

    n_slab = ATTN_WIDTH // LANES
    qa = _dot(xn, w_ref[:, COL_QA:COL_KA])
    ka = _dot(xn, w_ref[:, COL_KA:COL_VA])
    ss_q = [_group_sum(qa[:, p * 2 * LANES:(p + 1) * 2 * LANES] ** 2, bd) for p in range(n_slab // 2)]
    ss_q = [ss_q[s // 2][:, (s % 2) * LANES:(s % 2 + 1) * LANES] for s in range(n_slab)]
    qa = [qa[:, s * LANES:(s + 1) * LANES] for s in range(n_slab)]
    ss_k = _group_sum(ka * ka, bd[0:LANES, 0:LANES])

    t, first, last = _seq_edges(tps)
    w_pool = w_ref[:, COL_POOL:COL_QA]
    u = _dot(xn, w_pool)
    xh = _rms(jnp.concatenate([xp_ref[...], xnx_ref[...]], axis=0), nw_ref[...]).astype(BF16)
    uh = _dot(xh, w_pool)
    ext = jnp.concatenate([jnp.where(first, 0.0, uh[0:HALO]), u, jnp.where(last, 0.0, uh[HALO:2 * HALO])], axis=0)
    w2 = ext + _shift_rows(ext, -1)
    w4 = _shift_rows(w2, -1) + _shift_rows(w2, 1)
    w8 = _shift_rows(w4, -2) + _shift_rows(w4, 2)
    w16 = _shift_rows(w8, -4) + _shift_rows(w8, 4)
    sl = slice(HALO, HALO + tm)
    grp = lax.broadcasted_iota(jnp.int32, (tm, POOL_WIDTH), 1) // POOL_GDIM
    seg = jnp.where(grp == 0, w2[sl], jnp.where(grp == 1, w4[sl], jnp.where(grp == 2, w8[sl], w16[sl])))
    half = jnp.where(grp == 0, 1, jnp.where(grp == 1, 2, jnp.where(grp == 2, 4, 8)))
    pos = t * tm + lax.broadcasted_iota(jnp.int32, (tm, POOL_WIDTH), 0)
    cnt = (jnp.minimum(pos + half, seq) - jnp.maximum(pos - half, 0)).astype(F32)
    pooled = (seg / cnt - u).astype(BF16)

    qw = qw_ref[...] * Q_SCALE
    q_rot = jnp.concatenate([head_norm_rope(qa[s], ss_q[s], qw) for s in range(n_slab)], axis=1)
    q_ref[...] = q_rot.T.astype(BF16)

    kr = head_norm_rope(ka, ss_k, kw_ref[...])
    ks = pltpu.roll(kr, HEAD_DIM, 1)
    k_ref[:, 0:128] = jnp.where(lo64, kr, 0.0).astype(BF16)
    k_ref[:, 128:256] = jnp.where(lo64, 0.0, ks).astype(BF16)
    k_ref[:, 256:384] = jnp.where(lo64, ks, 0.0).astype(BF16)
    k_ref[:, 384:512] = jnp.where(lo64, 0.0, kr).astype(BF16)

    vat = _dot(xn, w_ref[:, COL_VA:COL_GQKV]).T
    pad_row = lax.broadcasted_iota(jnp.int32, (V_ROWS - HEAD_DIM, tm), 0)
    ones_rows = jnp.where(pad_row == 0, 1.0, 0.0)
    vt_ref[...] = jnp.concatenate(
        [vat[0:HEAD_DIM], ones_rows, vat[HEAD_DIM:2 * HEAD_DIM], ones_rows], axis=0).astype(BF16)
    g_ref[...] = _dot(xn, w_ref[:, COL_GQKV:COL_Z])
    z_ref[...] = _dot(xn, w_ref[:, COL_Z:COL_GATES])
    gt_ref[...] = _dot(xn, w_ref[:, COL_GATES:N_CAT])
    yp_ref[...] = (_dot(pooled, bdw_ref[...]) * psc_ref[...]).astype(BF16)


def _inproj(x2d, norm_w, w_cat, cos_t, sin_t, qw, kw, bd256, bdw, pool_scale, seq, tm, layer):
    n_tok = x2d.shape[0]
    tps = seq // tm
    const = lambda i: (0, 0)
    row = lambda i: (i, 0)
    pos = lambda i: (i % tps, 0)
    return pl.pallas_call(
        functools.partial(_inproj_kernel, tps=tps, seq=seq),
        grid=(n_tok // tm,),
        in_specs=_halo_specs(tm, D_MODEL, n_tok) + [
            pl.BlockSpec((None, 1, D_MODEL), lambda i: (layer, 0, 0)),
            pl.BlockSpec((None, D_MODEL, N_CAT), lambda i: (layer, 0, 0)),
            pl.BlockSpec((tm, LANES), pos),
            pl.BlockSpec((tm, LANES), pos),
            pl.BlockSpec((1, LANES), const),
            pl.BlockSpec((1, LANES), const),
            pl.BlockSpec((2 * LANES, 2 * LANES), const),
            pl.BlockSpec((POOL_WIDTH, POOL_WIDTH), const),
            pl.BlockSpec((1, POOL_WIDTH), const),
        ],
        out_specs=[
            pl.BlockSpec((tm, POOL_WIDTH), row),
            pl.BlockSpec((ATTN_WIDTH, tm), lambda i: (0, i)),
            pl.BlockSpec((tm, 4 * LANES), row),
            pl.BlockSpec((ATTN_KV_HEADS * V_ROWS, tm), lambda i: (0, i)),
            pl.BlockSpec((tm, 3 * GDN_WIDTH), row),
            pl.BlockSpec((tm, GDN_WIDTH), row),
            pl.BlockSpec((tm, LANES), row),
        ],
        out_shape=[
            jax.ShapeDtypeStruct((n_tok, POOL_WIDTH), BF16),
            jax.ShapeDtypeStruct((ATTN_WIDTH, n_tok), BF16),
            jax.ShapeDtypeStruct((n_tok, 4 * LANES), BF16),
            jax.ShapeDtypeStruct((ATTN_KV_HEADS * V_ROWS, n_tok), BF16),
            jax.ShapeDtypeStruct((n_tok, 3 * GDN_WIDTH), F32),
            jax.ShapeDtypeStruct((n_tok, GDN_WIDTH), F32),
            jax.ShapeDtypeStruct((n_tok, LANES), F32),
        ],
        compiler_params=_params(("parallel",)),
        name="inproj",
    )(x2d, x2d, x2d, norm_w, w_cat, cos_t, sin_t, qw, kw, bd256, bdw, pool_scale)


def _attn_kernel(q_ref, qn_ref, k_ref, vt_ref, o_ref, acc_ref, m_ref, s_ref, bm_ref, qq_ref, *, tk, per_iter):
    seq = k_ref.shape[0]
    group = ATTN_HEADS // ATTN_KV_HEADS
    n_chunks = seq // tk
    n_iter = n_chunks // per_iter
    acc_ref[...] = jnp.zeros_like(acc_ref)
    m_ref[...] = jnp.full_like(m_ref, -1e30)
    qq_ref[0] = q_ref[...]
    qq_ref[1] = qn_ref[...]

    def scores(c, buf, slab, tile=0):
        r0 = pl.multiple_of(c * tk, tk)
        kk = jnp.concatenate([k_ref[pl.ds(r0, tk), 0:LANES], k_ref[pl.ds(r0, tk), LANES:2 * LANES]], axis=0)
        s2 = _dot(kk, qq_ref[tile, slab * LANES:(slab + 1) * LANES, :])
        for half in range(2):
            j = 2 * slab + half
            s = s2[half * tk:(half + 1) * tk]
            s_ref[buf, j] = s
            bm_ref[buf, j] = jnp.max(s, axis=0, keepdims=True)

    def softmax_pv(c, buf, slab):
        r0 = pl.multiple_of(c * tk, tk)
        vt = vt_ref[:, pl.ds(r0, tk)]
        for j in (2 * slab, 2 * slab + 1):
            m = m_ref[j]
            m_new = jnp.maximum(m, bm_ref[buf, j])
            alpha = jnp.exp2(m - m_new)
            p = jnp.exp2(s_ref[buf, j] - m_new)
            m_ref[j] = m_new
            acc_ref[j] = alpha * acc_ref[j] + _dot(vt, p.astype(BF16))

    slabs = range(group // 2)

    @pl.when(pl.program_id(2) == 0)
    def _():
        for slab in slabs:
            scores(0, 0, slab)

    def body(i, carry):
        c0 = per_iter * i
        wrap = jnp.where(i == n_iter - 1, 1, 0)
        for u in range(per_iter):
            for slab in slabs:
                if u == per_iter - 1:
                    scores((1 - wrap) * (c0 + per_iter), 0, slab, tile=wrap)
                else:
                    scores(c0 + u + 1, (u + 1) % 2, slab)
            for slab in slabs:
                softmax_pv(c0 + u, u % 2, slab)
        return carry

    lax.fori_loop(0, n_iter, body, 0)
    out = jnp.concatenate([acc_ref[j, 0:HEAD_DIM] / acc_ref[j, HEAD_DIM:HEAD_DIM + 1] for j in range(group)], axis=0)
    o_ref[...] = out.T.astype(BF16)


def _attention(qt, k4, vt, batch, seq, tq, tk, per_iter):
    n_tok = k4.shape[0]
    nq = seq // tq
    group = ATTN_HEADS // ATTN_KV_HEADS
    gw = group * HEAD_DIM
    per_iter = min(per_iter, seq // tk)
    assert per_iter % 2 == 0 and (seq // tk) % per_iter == 0
    return pl.pallas_call(
        functools.partial(_attn_kernel, tk=tk, per_iter=per_iter),
        grid=(batch, ATTN_KV_HEADS, nq),
        in_specs=[
            pl.BlockSpec((gw, tq), lambda b, g, i: (g, b * nq + i)),
            pl.BlockSpec((gw, tq), lambda b, g, i: (g, b * nq + jnp.minimum(i + 1, nq - 1))),
            pl.BlockSpec((seq, 2 * LANES), lambda b, g, i: (b, g)),
            pl.BlockSpec((V_ROWS, seq), lambda b, g, i: (g, b)),
        ],
        out_specs=pl.BlockSpec((tq, gw), lambda b, g, i: (b * nq + i, g)),
        out_shape=jax.ShapeDtypeStruct((n_tok, ATTN_WIDTH), BF16),
        scratch_shapes=[pltpu.VMEM((group, V_ROWS, tq), F32),
                        pltpu.VMEM((group, 1, tq), F32),
                        pltpu.VMEM((2, group, tk, tq), F32),
                        pltpu.VMEM((2, group, 1, tq), F32),
                        pltpu.VMEM((2, gw, tq), BF16)],
        compiler_params=_params(("parallel", "parallel", "arbitrary")),
        name="attention",
    )(qt, qt, k4, vt)


def _gdn_prep_kernel(prev_ref, x_ref, next_ref, gt_ref, cw_ref, na_ref, dtb_ref, bd_ref, ex_ref,
                     q_ref, k_ref, v_ref, be_ref, gce_ref, gcr_ref, kgt_ref, *, tps):
    tm = x_ref.shape[0]
    _, first, last = _seq_edges(tps)
    prev = jnp.where(first, 0.0, prev_ref[...])
    nxt = jnp.where(last, 0.0, next_ref[...])
    ext = jnp.concatenate([prev, x_ref[...], nxt], axis=0)
    cw = cw_ref[...]
    conv = ext * cw[2:3]
    for kk in (0, 1, 3, 4):
        conv = conv + _shift_rows(ext, kk - GDN_CONV // 2) * cw[kk:kk + 1]
    conv = conv[HALO:HALO + tm]
    act = conv / (1.0 + jnp.exp(-conv))
    bd = bd_ref[...]

    def l2n(x):
        return x * lax.rsqrt(_group_sum(x * x, bd) + EPS)

    q = l2n(act[:, 0:GDN_WIDTH]) * (GDN_DK ** -0.5)
    k = l2n(act[:, GDN_WIDTH:2 * GDN_WIDTH])
    q_ref[...] = q
    k_ref[...] = k
    v_ref[...] = act[:, 2 * GDN_WIDTH:3 * GDN_WIDTH]

    gt = gt_ref[...]
    lane = lax.broadcasted_iota(jnp.int32, (tm, LANES), 1)
    beta = 1.0 / (1.0 + jnp.exp(-gt))
    xs = gt + dtb_ref[...]
    softplus = jnp.maximum(xs, 0.0) + jnp.log1p(jnp.exp(-jnp.abs(xs)))
    g = na_ref[...] * softplus

    r = lax.broadcasted_iota(jnp.int32, (tm, tm), 0)
    c = lax.broadcasted_iota(jnp.int32, (tm, tm), 1)
    same = (r // GDN_CHUNK) == (c // GDN_CHUNK)
    ones_bd = jnp.where(same, 1.0, 0.0).astype(BF16)
    tri_f = jnp.where(same & (c <= r), 1.0, 0.0).astype(BF16)
    tri_b = jnp.where(same & (c >= r), 1.0, 0.0).astype(BF16)
    gc = jnp.where(lane < 12, _sel_dot(tri_f, g), _sel_dot(tri_b, g))
    comp = jnp.where(lane < 8, beta, gc)
    exp_all = _dot_sel(comp, ex_ref[...])
    tot = _sel_dot(ones_bd, g)
    tot_e = _dot_sel(tot, ex_ref[:, 2 * GDN_WIDTH:4 * GDN_WIDTH])

    row = lax.broadcasted_iota(jnp.int32, (tm, GDN_WIDTH), 0)
    col = lax.broadcasted_iota(jnp.int32, (tm, GDN_WIDTH), 1)
    diag = (row % GDN_CHUNK) == (col % GDN_DK)
    for d in range(N_DIR):
        be_ref[d] = exp_all[:, d * GDN_WIDTH:(d + 1) * GDN_WIDTH]
        gce = exp_all[:, (2 + d) * GDN_WIDTH:(3 + d) * GDN_WIDTH]
        gce_ref[d] = gce
        gcr_ref[d] = _sel_dot(ones_bd, jnp.where(diag, gce, 0.0))
        kg = k * jnp.exp(tot_e[:, d * GDN_WIDTH:(d + 1) * GDN_WIDTH] - gce)
        lo_lane = lax.broadcasted_iota(jnp.int32, (GDN_DK, LANES), 1) < GDN_DK
        for pr in range(tm // LANES):
            xt = kg[pr * LANES:(pr + 1) * LANES, :].T
            xr = pltpu.roll(xt, GDN_CHUNK, 1)
            for cl in range(2):
                even, odd = (xt, xr) if cl == 0 else (xr, xt)
                tiles = [jnp.where(lo_lane, even[(2 * p) * GDN_DK:(2 * p + 1) * GDN_DK],
                                   odd[(2 * p + 1) * GDN_DK:(2 * p + 2) * GDN_DK]) for p in range(GDN_HEADS // 2)]
                r0 = (2 * pr + cl) * GDN_CHUNK
                kgt_ref[d, r0:r0 + GDN_CHUNK, :] = jnp.concatenate(tiles, axis=1).astype(BF16)


def _gdn_prep(gqkv, gates, conv_w, neg_a, dt_b, bd256, expand, seq, tm):
    n_tok = gqkv.shape[0]
    width = 3 * GDN_WIDTH
    const = lambda i: (0, 0)
    row = lambda i: (i, 0)
    drow = lambda i: (0, i, 0)
    tok = jax.ShapeDtypeStruct((n_tok, GDN_WIDTH), F32)
    dtok = jax.ShapeDtypeStruct((N_DIR, n_tok, GDN_WIDTH), F32)
    return pl.pallas_call(
        functools.partial(_gdn_prep_kernel, tps=seq // tm),
        grid=(n_tok // tm,),
        in_specs=_halo_specs(tm, width, n_tok) + [
            pl.BlockSpec((tm, LANES), row),
            pl.BlockSpec((GDN_CONV, width), const),
            pl.BlockSpec((1, LANES), const),
            pl.BlockSpec((1, LANES), const),
            pl.BlockSpec((GDN_WIDTH, GDN_WIDTH), const),
            pl.BlockSpec((LANES, 4 * GDN_WIDTH), const),
        ],
        out_specs=[
            pl.BlockSpec((tm, GDN_WIDTH), row),
            pl.BlockSpec((tm, GDN_WIDTH), row),
            pl.BlockSpec((tm, GDN_WIDTH), row),
            pl.BlockSpec((N_DIR, tm, GDN_WIDTH), drow),
            pl.BlockSpec((N_DIR, tm, GDN_WIDTH), drow),
            pl.BlockSpec((N_DIR, tm, GDN_WIDTH), drow),
            pl.BlockSpec((N_DIR, tm, GDN_WIDTH), drow),
        ],
        out_shape=[tok, tok, tok, dtok, dtok, dtok,
                   jax.ShapeDtypeStruct((N_DIR, n_tok, GDN_WIDTH), BF16)],
        compiler_params=_params(("parallel",)),
        name="gdn_prep",
    )(gqkv, gqkv, gqkv, gates, conv_w, neg_a, dt_b, bd256, expand)


def _gdn_scan_kernel(qf_ref, kf_ref, vf_ref, bef_ref, gcef_ref, gcrf_ref,
                     qb_ref, kb_ref, vb_ref, beb_ref, gceb_ref, gcrb_ref, kgpf_ref, kgpb_ref,
                     of_ref, ob_ref, state_ref, stage_ref, gl_ref, *, chunks, wave):
    cs = GDN_CHUNK
    w = GDN_WIDTH
    par = pl.program_id(1) % 2
    prv = 1 - par

    @pl.when(pl.program_id(1) == 0)
    def _():
        state_ref[...] = jnp.zeros_like(state_ref)
        stage_ref[...] = jnp.zeros_like(stage_ref)
        gl_ref[...] = jnp.zeros_like(gl_ref)

    lane_h = lax.broadcasted_iota(jnp.int32, (cs, w), 1) // GDN_DK
    heads = [lane_h == h for h in range(GDN_HEADS)]
    ti = lax.broadcasted_iota(jnp.int32, (cs, w), 0)
    tj = lax.broadcasted_iota(jnp.int32, (cs, w), 1) % cs
    eye = (ti == tj).astype(F32)
    in_refs = ((qf_ref, kf_ref, vf_ref, bef_ref, gcef_ref, gcrf_ref),
               (qb_ref, kb_ref, vb_ref, beb_ref, gceb_ref, gcrb_ref))
    kgp_refs = (kgpf_ref, kgpb_ref)
    o_refs = (of_ref, ob_ref)
    U, W, QG, QKM = range(4)

    def blockdiag(y):
        return jnp.concatenate([jnp.where(hm, y, 0.0) for hm in heads], axis=0).astype(BF16)

    def rows_of(ci):
        return slice(ci * cs, (ci + 1) * cs)

    loc = {}

    def local_scores(d, ci):
        q_ref, k_ref, v_ref, be_ref, gce_ref, gcr_ref = in_refs[d]
        rs = rows_of(ci)
        q = q_ref[rs, :]
        k = k_ref[rs, :]
        be = be_ref[0, rs, :]
        kb = k * be
        kq = _dot_nt(jnp.concatenate([kb, q], axis=0).astype(BF16), blockdiag(k))
        loc[d, ci] = dict(kq=kq)

    def local_decay(d, ci):
        st = loc[d, ci]
        _, _, _, _, gce_ref, gcr_ref = in_refs[d]
        rs = rows_of(ci)
        incl = (ti >= tj) if d == 0 else (ti <= tj)
        strict = (ti > tj) if d == 0 else (ti < tj)
        diff = gce_ref[0, rs, :] - gcr_ref[0, rs, :]
        decay = jnp.where(incl, jnp.exp(jnp.where(incl, diff, 0.0)), 0.0)
        kq = st.pop("kq")
        a = jnp.where(strict, kq[0:cs] * decay, 0.0)
        stage_ref[par, d, QKM, rs, :] = kq[cs:2 * cs] * decay
        st["pw"] = _dot(a.astype(BF16), blockdiag(a))
        st["t"] = eye - a

    def local_invert(d, ci, last):
        st = loc[d, ci]
        bdp = blockdiag(st["pw"])
        if last:
            st["t"] = st["t"] + _dot(st["t"].astype(BF16), bdp)
        else:
            both = _dot(jnp.concatenate([st["pw"], st["t"]], axis=0).astype(BF16), bdp)
            st["pw"] = both[0:cs]
            st["t"] = st["t"] + both[cs:2 * cs]

    def local_uw(d, ci):
        st = loc.pop((d, ci))
        q_ref, k_ref, v_ref, be_ref, gce_ref, _ = in_refs[d]
        rs = rows_of(ci)
        eg = jnp.exp(gce_ref[0, rs, :])
        be = be_ref[0, rs, :]
        rhs = jnp.concatenate([blockdiag(v_ref[rs, :] * be), blockdiag(k_ref[rs, :] * be * eg)], axis=1)
        uw = _dot(st["t"].astype(BF16), rhs)
        stage_ref[par, d, U, rs, :] = uw[:, 0:w]
        stage_ref[par, d, W, rs, :] = uw[:, w:2 * w]
        stage_ref[par, d, QG, rs, :] = q_ref[rs, :] * eg
        edge = (ci + 1) * cs - 1 if d == 0 else ci * cs
        gl_ref[par, d, ci] = jnp.exp(gce_ref[0, edge:edge + 1, :])

    scan = {}

    def scan_read(d, ci):
        rs = rows_of(ci)
        lhs = jnp.concatenate([stage_ref[prv, d, W, rs, :], stage_ref[prv, d, QG, rs, :]], axis=0)
        scan[d] = _dot(lhs.astype(BF16), blockdiag(state_ref[d]))

    def scan_out(d, ci):
        rs = rows_of(ci)
        ws = scan[d]
        vn = stage_ref[prv, d, U, rs, :] - ws[0:cs]
        lhs = jnp.concatenate([stage_ref[prv, d, QKM, rs, :].astype(BF16), kgp_refs[d][0, rs, :]], axis=0)
        res = _dot(lhs, blockdiag(vn))
        o_refs[d][rs, :] = ws[cs:2 * cs] + res[0:cs]
        state_ref[d] = state_ref[d] * gl_ref[prv, d, ci] + res[cs:2 * cs]

    order = [(d, ci if d == 0 else chunks - 1 - ci) for ci in range(chunks) for d in range(N_DIR)]
    local_ops = []
    for w0 in range(0, len(order), wave):
        wv = order[w0:w0 + wave]
        local_ops += [functools.partial(local_scores, d, ci) for d, ci in wv]
        local_ops += [functools.partial(local_decay, d, ci) for d, ci in wv]
        for step in range(5):
            local_ops += [functools.partial(local_invert, d, ci, step == 4) for d, ci in wv]
        local_ops += [functools.partial(local_uw, d, ci) for d, ci in wv]
    scan_ops = []
    for ci in range(chunks):
        for fn in (scan_read, scan_out):
            scan_ops.append([functools.partial(fn, d, ci if d == 0 else chunks - 1 - ci) for d in range(N_DIR)])
    every = len(local_ops) // len(scan_ops)
    for idx, op in enumerate(local_ops):
        op()
        if (idx + 1) % every == 0 and scan_ops:
            for s_op in scan_ops.pop(0):
                s_op()
    for level in scan_ops:
        for s_op in level:
            s_op()


def _gdn_scan(q, k, v, be, gce, gcr, kgp, batch, seq, chunks):
    n_tok = q.shape[0]
    rows = chunks * GDN_CHUNK
    ng = seq // rows
    w = GDN_WIDTH

    def group(d, lag):
        def f(b, n):
            g = jnp.clip(n - lag, 0, ng - 1)
            return b * ng + (ng - 1 - g if d == 1 else g)
        return f

    def tok(d, lag):
        f = group(d, lag)
        return lambda b, n: (f(b, n), 0)

    def dtok(d, lag):
        f = group(d, lag)
        return lambda b, n: (d, f(b, n), 0)

    def local_specs(d):
        return [pl.BlockSpec((rows, w), tok(d, 0))] * 3 + [pl.BlockSpec((1, rows, w), dtok(d, 0))] * 3

    out = jax.ShapeDtypeStruct((n_tok, w), F32)
    return pl.pallas_call(
        functools.partial(_gdn_scan_kernel, chunks=chunks, wave=2 * chunks),
        grid=(batch, ng + 1),
        in_specs=local_specs(0) + local_specs(1) + [pl.BlockSpec((1, rows, w), dtok(d, 1)) for d in range(N_DIR)],
        out_specs=[pl.BlockSpec((rows, w), tok(d, 1)) for d in range(N_DIR)],
        out_shape=[out, out],
        scratch_shapes=[pltpu.VMEM((N_DIR, GDN_DK, w), F32),
                        pltpu.VMEM((2, N_DIR, 4, rows, w), F32),
                        pltpu.VMEM((2, N_DIR, chunks, 1, w), F32)],
        compiler_params=_params(("parallel", "arbitrary")),
        name="gdn_scan",
    )(q, k, v, be, gce, gcr, q, k, v, be, gce, gcr, kgp, kgp)


def _mixffn_kernel(yp_p, yp_m, yp_n, ya_p, ya_m, ya_n, of_p, of_m, of_n, ob_p, ob_m, ob_n,
                   z_p, z_m, z_n, h_p, h_m, h_n, gw_ref, bd_ref, wo_ref, postmix_ref,
                   prew_ref, wg_ref, wv_ref, cg_ref, cv_ref, wd_ref, postw_ref, o_ref, act_ref, *, tps, chunk):
    tm = h_m.shape[0]
    _, first, last = _seq_edges(tps)

    def ext(p_ref, m_ref, n_ref):
        p = p_ref[...].astype(F32)
        n = n_ref[...].astype(F32)
        return jnp.concatenate([p[p.shape[0] - HALO:], m_ref[...].astype(F32), n[0:HALO]], axis=0)

    a0 = POOL_WIDTH
    a1 = POOL_WIDTH + ATTN_WIDTH
    mix = _dot(ext(yp_p, yp_m, yp_n).astype(BF16), wo_ref[0:a0, :])
    mix = mix + _dot(ext(ya_p, ya_m, ya_n).astype(BF16), wo_ref[a0:a1, :])
    o = ext(of_p, of_m, of_n) + ext(ob_p, ob_m, ob_n)
    ms = _group_sum(o * o, bd_ref[...]) * (1.0 / GDN_DK)
    z = ext(z_p, z_m, z_n)
    yg = o * lax.rsqrt(ms + EPS) * gw_ref[...] * (z / (1.0 + jnp.exp(-z)))
    mix = mix + _dot(yg.astype(BF16), wo_ref[a1:D_MODEL, :])
    h1 = ext(h_p, h_m, h_n) + _rms(mix, postmix_ref[...])

    row = lax.broadcasted_iota(jnp.int32, (tm + 2 * HALO, 1), 0)
    outside = (first & (row < HALO)) | (last & (row >= HALO + tm))
    xn = jnp.where(outside, 0.0, _rms(h1, prew_ref[...])).astype(BF16)
    sl = slice(HALO, HALO + tm)

    def conv3(up, cw):
        out = _shift_rows(up, -1) * cw[0:1] + up * cw[1:2] + _shift_rows(up, 1) * cw[2:3]
        return out[sl]

    for c in range(D_FF // chunk):
        cs = slice(c * chunk, (c + 1) * chunk)
        gate = conv3(_dot(xn, wg_ref[:, cs]), cg_ref[:, cs])
        val = conv3(_dot(xn, wv_ref[:, cs]), cv_ref[:, cs])
        inner = 0.7978845608028654 * (gate + 0.044715 * (gate * gate * gate))
        act_ref[:, cs] = (0.5 * gate * (1.0 + jnp.tanh(inner)) * val).astype(BF16)
    ff = _dot(act_ref[...], wd_ref[...])
    o_ref[...] = h1[sl] + _rms(ff, postw_ref[...])


def _mixffn(y_pool, y_attn, o_f, o_b, z, h, gdn_nw, bd256, w_out, post_mix_w,
            pre_w, w_up, conv_w, w_down, post_w, seq, tm, chunk, layer):
    n_tok = h.shape[0]
    const = lambda i: (0, 0)
    mat = lambda i: (layer, 0, 0)
    second = lambda i: (layer, 0, 1)
    resident = pl.Buffered(1)

    def halo(width, rows):
        per = tm // rows
        last = n_tok // rows - 1
        return [
            pl.BlockSpec((rows, width), lambda i: (jnp.maximum(i * per - 1, 0), 0)),
            pl.BlockSpec((tm, width), lambda i: (i, 0)),
            pl.BlockSpec((rows, width), lambda i: (jnp.minimum((i + 1) * per, last), 0)),
        ]

    bf16_rows = 2 * SUBLANES
    in_specs = (halo(POOL_WIDTH, bf16_rows) + halo(ATTN_WIDTH, bf16_rows) + halo(GDN_WIDTH, HALO) * 3
                + halo(D_MODEL, HALO) + [
        pl.BlockSpec((1, GDN_WIDTH), const),
        pl.BlockSpec((GDN_WIDTH, GDN_WIDTH), const),
        pl.BlockSpec((None, D_MODEL, D_MODEL), mat, pipeline_mode=resident),
        pl.BlockSpec((None, 1, D_MODEL), mat),
        pl.BlockSpec((None, 1, D_MODEL), mat),
        pl.BlockSpec((None, D_MODEL, D_FF), mat, pipeline_mode=resident),
        pl.BlockSpec((None, D_MODEL, D_FF), second, pipeline_mode=resident),
        pl.BlockSpec((None, FFN_CONV, D_FF), mat),
        pl.BlockSpec((None, FFN_CONV, D_FF), second),
        pl.BlockSpec((None, D_FF, D_MODEL), mat, pipeline_mode=resident),
        pl.BlockSpec((None, 1, D_MODEL), mat),
    ])
    return pl.pallas_call(
        functools.partial(_mixffn_kernel, tps=seq // tm, chunk=chunk),
        grid=(n_tok // tm,),
        in_specs=in_specs,
        out_specs=pl.BlockSpec((tm, D_MODEL), lambda i: (i, 0)),
        out_shape=jax.ShapeDtypeStruct((n_tok, D_MODEL), F32),
        scratch_shapes=[pltpu.VMEM((tm, D_FF), BF16)],
        compiler_params=_params(("parallel",)),
        name="mixffn",
    )(y_pool, y_pool, y_pool, y_attn, y_attn, y_attn, o_f, o_f, o_f, o_b, o_b, o_b, z, z, z, h, h, h,
      gdn_nw, bd256, w_out, post_mix_w, pre_w, w_up, w_up, conv_w, conv_w, w_down, post_w)


def _rope_tables(seq):
    t = np.arange(seq)
    pos = np.stack([t // GRID_W, t % GRID_W], axis=-1).astype(np.float32)
    axis_dim = HEAD_DIM // 2
    inv_freq = (ROPE_THETA ** (-np.arange(0, axis_dim, 2, dtype=np.float32) / axis_dim)).astype(np.float32)
    ang = pos[:, :, None] * inv_freq
    j = np.arange(LANES) % HEAD_DIM
    a = ang[:, j // 32, j % 16]
    sign = np.where((j % 32) < 16, -1.0, 1.0).astype(np.float32)
    return jnp.asarray(np.cos(a), F32), jnp.asarray(np.sin(a) * sign, F32)


def _head_blockdiag(n):
    i = np.arange(n) // HEAD_DIM
    return jnp.asarray(i[:, None] == i[None, :], BF16)


def _gate_expand():
    e = np.zeros((LANES, 4 * GDN_WIDTH), np.float32)
    for kind in range(2):
        for d in range(N_DIR):
            for hh in range(GDN_HEADS):
                src = kind * N_DIR * GDN_HEADS + d * GDN_HEADS + hh
                dst = (kind * N_DIR + d) * GDN_WIDTH + hh * GDN_DK
                e[src, dst:dst + GDN_DK] = 1.0
    return jnp.asarray(e, BF16)


def _pad_lanes(v, offset):
    out = jnp.zeros((1, LANES), F32)
    return out.at[0, offset:offset + v.shape[0]].set(v.astype(F32))


def kernel(x, pre_mix_norm, post_mix_norm, pre_ffn_norm, post_ffn_norm, w_in, pool_w, pool_scale, q_norm, k_norm, gdn_conv, gdn_a_log, gdn_dt_bias, gdn_norm, w_out, w_up, ffn_conv, w_down):
    batch, seq, _ = x.shape
    n_tok = batch * seq
    depth = w_in.shape[0]
    assert seq % 512 == 0 and seq % GRID_W == 0

    tm_proj = 512
    tm_gdn = 256
    tm_ffn = 512
    cos_t, sin_t = _rope_tables(seq)
    bd256 = _head_blockdiag(GDN_WIDTH)
    expand = _gate_expand()
    n_gate = N_DIR * GDN_HEADS

    w_cat = jnp.pad(w_in, ((0, 0), (0, 0), (0, N_CAT - D_IN))).astype(BF16)
    w_out_b = w_out.astype(BF16)
    w_up_b = w_up.astype(BF16)
    w_down_b = w_down.astype(BF16)

    h = x.reshape(n_tok, D_MODEL)
    for l in range(depth):
        qw = jnp.tile(q_norm[l], 2)[None, :]
        kw = jnp.tile(k_norm[l], 2)[None, :]
        bdw = jax.scipy.linalg.block_diag(*[pool_w[l, g] for g in range(POOL_GROUPS)]).astype(BF16)
        y_pool, qt, k4, vt, gqkv, z, gates = _inproj(
            h, pre_mix_norm[:, None, :], w_cat, cos_t, sin_t, qw, kw, bd256, bdw, pool_scale[l][None, :],
            seq, tm_proj, l)

        y_attn = _attention(qt, k4, vt, batch, seq, 256, 256, 16)

        neg_a = _pad_lanes(-jnp.exp(gdn_a_log[l].reshape(-1)), n_gate)
        dt_b = _pad_lanes(gdn_dt_bias[l].reshape(-1), n_gate)
        gq, gk, gv, be, gce, gcr, kgt = _gdn_prep(gqkv, gates, gdn_conv[l], neg_a, dt_b, bd256, expand, seq, tm_gdn)
        o_f, o_b = _gdn_scan(gq, gk, gv, be, gce, gcr, kgt, batch, seq, 8)

        h = _mixffn(y_pool, y_attn, o_f, o_b, z, h, jnp.tile(gdn_norm[l], GDN_HEADS)[None, :], bd256,
                    w_out_b, post_mix_norm[:, None, :], pre_ffn_norm[:, None, :], w_up_b, ffn_conv, w_down_b,
                    post_ffn_norm[:, None, :], seq, tm_ffn, 256, l)
    return h.reshape(batch, seq, D_MODEL)
```
